```python
import math
import jax
import jax.numpy as jnp
from jax import lax
import numpy as np

D_MODEL = 1024
BATCH = 4
SEQ = 4096
DEPTH = 4

GRID_W = 64
CTX_LEN = 256
N_GROUPS = 4
D_MIX = D_MODEL
GROUP_W = D_MIX // N_GROUPS
HEAD_DIM = 64
GROUP_HEADS = GROUP_W // HEAD_DIM
D_FF = 4 * D_MODEL
CHUNK = 64
RWKV_DECAY_RANK = 64
RWKV_AAA_RANK = 64
RWKV_GATE_RANK = 128
GDN_CONV_W = 5
LN_EPS = 1e-5
RWKV_GN_EPS = 64e-5
NORM_EPS = 1e-6
DEEPNORM_ALPHA = (2.0 * DEPTH) ** 0.25
DEEPNORM_BETA = (8.0 * DEPTH) ** -0.25
A_COLS = (GROUP_W, GROUP_W, 2 * GROUP_W, GROUP_W)
B_COLS = (GROUP_W, GROUP_W, GROUP_W, 2 * RWKV_DECAY_RANK, 2 * RWKV_AAA_RANK, RWKV_GATE_RANK)
C_COLS = (3 * GROUP_W, 2 * GROUP_HEADS, 2 * GROUP_HEADS, GROUP_W)
D_COLS = (3 * GROUP_W, 2 * GROUP_HEADS, 2 * GROUP_HEADS, GROUP_W)
GROUP_COLS = (sum(A_COLS), sum(B_COLS), sum(C_COLS), sum(D_COLS))
D_IN = sum(GROUP_COLS)

kernel_name = "hybrid_parallel_heads_flow_block"


def split_cols(t, widths):
    offsets = [int(o) for o in np.cumsum(widths)[:-1]]
    return jnp.split(t, offsets, axis=-1)


def layer_norm(t, g, b):
    tf = t.astype(jnp.float32)
    mu = jnp.mean(tf, -1, keepdims=True)
    var = jnp.mean(jnp.square(tf - mu), -1, keepdims=True)
    return ((tf - mu) * lax.rsqrt(var + LN_EPS) * g + b).astype(t.dtype)


def modulate(t, shift, scale):
    return t * (1 + scale) + shift


def squared_relu_mlp(h, w1, w2):
    return jnp.square(jax.nn.relu(h @ w1)) @ w2


def to_col_major(t):
    bsz, n, ch = t.shape
    rows = n // GRID_W
    return t.reshape(bsz, rows, GRID_W, ch).transpose(0, 2, 1, 3).reshape(bsz, n, ch)


def from_col_major(t):
    bsz, n, ch = t.shape
    rows = n // GRID_W
    return t.reshape(bsz, GRID_W, rows, ch).transpose(0, 2, 1, 3).reshape(bsz, n, ch)


def split_heads(t):
    return t.reshape(t.shape[:-1] + (GROUP_HEADS, HEAD_DIM))


def heads(t):
    return jnp.swapaxes(split_heads(t), -2, -3)


def unheads(t):
    t = jnp.swapaxes(t, -2, -3)
    return t.reshape(t.shape[:-2] + (GROUP_W,))


def head_scalars(t):
    return jnp.swapaxes(t, -1, -2)


def l2_normalize(t):
    return t * lax.rsqrt(jnp.sum(t * t, -1, keepdims=True) + 1e-12)


def head_l2(t):
    return l2_normalize(split_heads(t)).reshape(t.shape)


def head_rms_norm(t, g):
    th = split_heads(t)
    th = th * lax.rsqrt(jnp.mean(th * th, -1, keepdims=True) + NORM_EPS)
    return th.reshape(t.shape) * g


def group_norm(t, g, b):
    mu = jnp.mean(t, -1, keepdims=True)
    var = jnp.mean(jnp.square(t - mu), -1, keepdims=True)
    tn = (t - mu) * lax.rsqrt(var + RWKV_GN_EPS)
    return tn.reshape(t.shape[:-2] + (GROUP_W,)) * g + b


def dir_shared(t):
    return jnp.stack([t, jnp.flip(t, axis=1)])


def dir_own(t):
    fwd, bwd = jnp.split(t, 2, axis=-1)
    return jnp.stack([fwd, jnp.flip(bwd, axis=1)])


def dir_sum(t):
    return t[0] + jnp.flip(t[1], axis=1)


def to_chunks(t, axis):
    n = t.shape[axis]
    t = t.reshape(t.shape[:axis] + (n // CHUNK, CHUNK) + t.shape[axis + 1:])
    return jnp.moveaxis(t, axis, 0)


def from_chunks(t):
    t = jnp.moveaxis(t, 0, -3)
    return t.reshape(t.shape[:-3] + (t.shape[-3] * t.shape[-2], t.shape[-1]))


def token_shift(t, mu):
    pad = jnp.pad(t, ((0, 0), (1, 1), (0, 0)))
    return t + mu * (0.5 * (pad[:, :-2] + pad[:, 2:]) - t)


def depthwise_conv(t, w):
    k = w.shape[0]
    return lax.conv_general_dilated(t, w.astype(t.dtype)[:, None, :], window_strides=(1,),
                                    padding=[(k // 2, k // 2)],
                                    dimension_numbers=('NWC', 'WIO', 'NWC'),
                                    feature_group_count=t.shape[-1])


def chunk_gla(q, k, v, log_f, s0):
    causal = jnp.tril(jnp.ones((CHUNK, CHUNK), bool))[:, :, None]

    def step(s, inp):
        qc, kc, vc, lfc = inp
        cum = jnp.cumsum(lfc, axis=-2)
        rel = cum[..., :, None, :] - cum[..., None, :, :]
        decay = jnp.exp(jnp.where(causal, rel, -jnp.inf))
        scores = jnp.einsum('...ik,...ijk,...jk->...ij', qc, decay, kc)
        o = (jnp.einsum('...ik,...kv->...iv', qc * jnp.exp(cum), s)
             + jnp.einsum('...ij,...jv->...iv', scores, vc))
        last = cum[..., -1:, :]
        s = (jnp.exp(last[..., 0, :])[..., :, None] * s
             + jnp.einsum('...jk,...jv->...kv', kc * jnp.exp(last - cum), vc))
        return s, o

    xs = tuple(to_chunks(t, t.ndim - 2) for t in (q, k, v, log_f))
    s, o = lax.scan(step, s0, xs)
    return from_chunks(o), s


def chunk_gated_delta(q, k, v, beta, log_a, s0):
    eye = jnp.eye(CHUNK, dtype=jnp.float32)
    incl = jnp.tril(jnp.ones((CHUNK, CHUNK), bool))
    strict = jnp.tril(jnp.ones((CHUNK, CHUNK), bool), -1)

    def step(s, inp):
        qc, kc, vc, bc, lac = inp
        cum = jnp.cumsum(lac, axis=-1)
        gam = jnp.exp(jnp.where(incl, cum[..., :, None] - cum[..., None, :], -jnp.inf))
        kb = kc * bc[..., None]
        m = eye + jnp.where(strict, jnp.einsum('...id,...jd->...ij', kb, kc) * gam, 0.0)
        rhs = jnp.concatenate([vc * bc[..., None], kb * jnp.exp(cum)[..., None]], axis=-1)
        sol = lax.linalg.triangular_solve(m, rhs, left_side=True, lower=True, unit_diagonal=True)
        u, w = sol[..., :HEAD_DIM], sol[..., HEAD_DIM:]
        v_new = u - jnp.einsum('...ik,...kv->...iv', w, s)
        attn = jnp.einsum('...id,...jd->...ij', qc, kc) * gam
        o = (jnp.einsum('...ik,...kv->...iv', qc * jnp.exp(cum)[..., None], s)
             + jnp.einsum('...ij,...jv->...iv', attn, v_new))
        last = cum[..., -1:]
        s = (jnp.exp(last)[..., None] * s
             + jnp.einsum('...jk,...jv->...kv', kc * jnp.exp(last - cum)[..., None], v_new))
        return s, o

    xs = (to_chunks(q, q.ndim - 2), to_chunks(k, k.ndim - 2), to_chunks(v, v.ndim - 2),
          to_chunks(beta, beta.ndim - 1), to_chunks(log_a, log_a.ndim - 1))
    s, o = lax.scan(step, s0, xs)
    return from_chunks(o), s


def chunk_mlstm(q, k, v, log_i, log_f, state):
    incl = jnp.tril(jnp.ones((CHUNK, CHUNK), bool))

    def step(carry, inp):
        cmat, nvec, m = carry
        qc, kc, vc, lic, lfc = inp
        b = jnp.cumsum(lfc, axis=-1)
        dlog = jnp.where(incl, b[..., :, None] - b[..., None, :] + lic[..., None, :], -jnp.inf)
        inter = b + m[..., None]
        m_i = jnp.maximum(inter, jnp.max(dlog, -1))
        dw = jnp.exp(dlog - m_i[..., None])
        iw = jnp.exp(inter - m_i)
        sc = jnp.einsum('...id,...jd->...ij', qc, kc) * dw
        num = (iw[..., None] * jnp.einsum('...ik,...kv->...iv', qc, cmat)
               + jnp.einsum('...ij,...jv->...iv', sc, vc))
        den = iw * jnp.einsum('...ik,...k->...i', qc, nvec) + jnp.sum(sc, -1)
        h = num / jnp.maximum(jnp.abs(den), jnp.exp(-m_i))[..., None]
        bl = b[..., -1]
        src = bl[..., None] - b + lic
        m_new = jnp.maximum(bl + m, jnp.max(src, -1))
        carry_w = jnp.exp(bl + m - m_new)
        wj = jnp.exp(src - m_new[..., None])
        cmat = carry_w[..., None, None] * cmat + jnp.einsum('...jk,...jv->...kv', kc * wj[..., None], vc)
        nvec = carry_w[..., None] * nvec + jnp.einsum('...jk,...j->...k', kc, wj)
        return (cmat, nvec, m_new), h

    xs = (to_chunks(q, q.ndim - 2), to_chunks(k, k.ndim - 2), to_chunks(v, v.ndim - 2),
          to_chunks(log_i, log_i.ndim - 1), to_chunks(log_f, log_f.ndim - 1))
    state, h = lax.scan(step, state, xs)
    return from_chunks(h), state


def rwkv7_scan(r, decay, k, v, kk, a, s0):
    def step(s, inp):
        rt, wt, kt, vt, kkt, at = inp
        sa = jnp.einsum('...vk,...k->...v', s, kkt)
        s = (s * wt[..., None, :] - sa[..., :, None] * (kkt * at)[..., None, :]
             + vt[..., :, None] * kt[..., None, :])
        return s, jnp.einsum('...vk,...k->...v', s, rt)

    xs = tuple(jnp.moveaxis(t, 2, 0) for t in (r, decay, k, v, kk, a))
    s, o = lax.scan(step, s0, xs)
    return jnp.moveaxis(o, 0, 2), s


def hgrn2_mixer(cols_ctx, cols_lat, gamma, layer, norm_g):
    lb_cum = jnp.cumsum(jax.nn.softmax(gamma.astype(jnp.float32), axis=0), axis=0)
    lb = (lb_cum[layer] - lb_cum[0])[:, None, None, :]

    def prep(cols):
        q, i, f, g = split_cols(cols.astype(jnp.float32), A_COLS)
        log_f = jnp.logaddexp(jnp.log(lb), jnp.log1p(-lb) + jax.nn.log_sigmoid(dir_own(f)))
        k = -jnp.expm1(log_f)
        return (heads(dir_shared(q)), heads(k), heads(dir_shared(i)), heads(log_f)), g

    def finish(o, g):
        return head_rms_norm(dir_sum(unheads(o)), norm_g) * jax.nn.silu(g)

    xs_ctx, g_ctx = prep(cols_ctx)
    s0 = jnp.zeros((2, cols_ctx.shape[0], GROUP_HEADS, HEAD_DIM, HEAD_DIM), jnp.float32)
    o_ctx, s_ctx = chunk_gla(*xs_ctx, s0)
    xs_lat, g_lat = prep(cols_lat)
    o_lat, _ = chunk_gla(*xs_lat, s_ctx)
    return finish(o_ctx, g_ctx), finish(o_lat, g_lat)


def rwkv7_mixer(cols_ctx, cols_lat, mu, w0, w2, a0, a2, g2, k_k, k_a, r_k, ln_g, ln_b):
    def prep(cols):
        cols = token_shift(cols.astype(jnp.float32), mu)
        r, k, v, wd, ad, gd = split_cols(cols, B_COLS)
        w = -jax.nn.softplus(-(w0[:, None, None, :] + jnp.einsum('dbnr,drc->dbnc', jnp.tanh(dir_own(wd)), w2))) - 0.5
        decay = jnp.exp(-jnp.exp(w))
        a = jax.nn.sigmoid(a0[:, None, None, :] + jnp.einsum('dbnr,drc->dbnc', dir_own(ad), a2))
        kk = l2_normalize(split_heads(k * k_k))
        k_dir = dir_shared(k) * (1 + (a - 1) * k_a)
        gate = jax.nn.sigmoid(gd) @ g2
        xs = (split_heads(dir_shared(r)), split_heads(decay), split_heads(k_dir),
              split_heads(dir_shared(v)), dir_shared(kk), split_heads(a))
        return xs, gate

    def finish(o, xs, gate):
        r2, _, k2, v2, _, _ = xs
        bonus = jnp.sum(r2 * k2 * r_k, -1, keepdims=True) * v2
        y = group_norm(dir_sum(o), ln_g, ln_b) + dir_sum(bonus).reshape(gate.shape)
        return y * gate

    xs_ctx, g_ctx = prep(cols_ctx)
    s0 = jnp.zeros((2, cols_ctx.shape[0], GROUP_HEADS, HEAD_DIM, HEAD_DIM), jnp.float32)
    o_ctx, s_ctx = rwkv7_scan(*xs_ctx, s0)
    xs_lat, g_lat = prep(cols_lat)
    o_lat, _ = rwkv7_scan(*xs_lat, s_ctx)
    return finish(o_ctx, xs_ctx, g_ctx), finish(o_lat, xs_lat, g_lat)


def gated_deltanet_mixer(cols_ctx, cols_lat, conv_w, a_log, dt_bias, norm_g):
    def prep(cols):
        qkv, beta, alpha, g = split_cols(cols.astype(jnp.float32), C_COLS)
        q, k, v = jnp.split(jax.nn.silu(depthwise_conv(qkv, conv_w)), 3, axis=-1)
        q = head_l2(q) * HEAD_DIM ** -0.5
        k = head_l2(k)
        beta = jax.nn.sigmoid(dir_own(beta))
        log_a = -jnp.exp(a_log)[:, None, None, :] * jax.nn.softplus(dir_own(alpha) + dt_bias[:, None, None, :])
        xs = (heads(dir_shared(q)), heads(dir_shared(k)), heads(dir_shared(v)),
              head_scalars(beta), head_scalars(log_a))
        return xs, g

    def finish(o, g):
        return head_rms_norm(dir_sum(unheads(o)), norm_g) * jax.nn.silu(g)

    xs_ctx, g_ctx = prep(cols_ctx)
    s0 = jnp.zeros((2, cols_ctx.shape[0], GROUP_HEADS, HEAD_DIM, HEAD_DIM), jnp.float32)
    o_ctx, s_ctx = chunk_gated_delta(*xs_ctx, s0)
    xs_lat, g_lat = prep(cols_lat)
    o_lat, _ = chunk_gated_delta(*xs_lat, s_ctx)
    return finish(o_ctx, g_ctx), finish(o_lat, g_lat)


def mlstm_mixer(cols_ctx, cols_lat, i_bias, f_bias, norm_g):
    def prep(cols):
        qkv, ig, fg, og = split_cols(cols.astype(jnp.float32), D_COLS)
        q, k, v = jnp.split(qkv, 3, axis=-1)
        k = k * HEAD_DIM ** -0.5
        log_i = dir_own(ig) + i_bias[:, None, None, :]
        log_f = jax.nn.log_sigmoid(dir_own(fg) + f_bias[:, None, None, :])
        xs = (heads(dir_shared(q)), heads(dir_shared(k)), heads(dir_shared(v)),
              head_scalars(log_i), head_scalars(log_f))
        return xs, og

    def finish(h, og):
        return head_rms_norm(dir_sum(unheads(h)), norm_g) * jax.nn.sigmoid(og)

    xs_ctx, o_gate_ctx = prep(cols_ctx)
    bsz = cols_ctx.shape[0]
    state0 = (jnp.zeros((2, bsz, GROUP_HEADS, HEAD_DIM, HEAD_DIM), jnp.float32),
              jnp.zeros((2, bsz, GROUP_HEADS, HEAD_DIM), jnp.float32),
              jnp.zeros((2, bsz, GROUP_HEADS), jnp.float32))
    h_ctx, state_ctx = chunk_mlstm(*xs_ctx, state0)
    xs_lat, o_gate_lat = prep(cols_lat)
    h_lat, _ = chunk_mlstm(*xs_lat, state_ctx)
    return finish(h_ctx, o_gate_ctx), finish(h_lat, o_gate_lat)


def setup_inputs(seed: int = 0) -> dict:
    key = jax.random.key(seed)
    keys = iter(jax.random.split(key, 40))

    def nrm(shape, scale):
        return scale * jax.random.normal(next(keys), shape, jnp.float32)

    def uni(shape, lo, hi):
        return jax.random.uniform(next(keys), shape, jnp.float32, lo, hi)

    D, G, H, L = D_MODEL, GROUP_W, GROUP_HEADS, DEPTH
    dt = jnp.exp(uni((L, 2, H), math.log(1e-3), math.log(1e-1)))
    return {
        "x": nrm((BATCH, SEQ, D), 1.0),
        "c": nrm((BATCH, D), 1.0),
        "ctx": nrm((BATCH, CTX_LEN, D), 1.0),
        "c_ctx": nrm((D,), 1.0),
        "ada_w": nrm((L, D, 6 * D), 0.5 * D ** -0.5),
        "ada_b": nrm((L, 6 * D), 0.02),
        "w_in": nrm((L, D, D_IN), D ** -0.5),
        "w_out": nrm((L, D_MIX, D), DEEPNORM_BETA * D_MIX ** -0.5),
        "ln1_g": 1.0 + nrm((L, D), 0.05),
        "ln1_b": nrm((L, D), 0.02),
        "ln2_g": 1.0 + nrm((L, D), 0.05),
        "ln2_b": nrm((L, D), 0.02),
        "mlp_w1": nrm((L, D, D_FF), D ** -0.5),
        "mlp_w2": nrm((L, D_FF, D), DEEPNORM_BETA * D_FF ** -0.5),
        "hgrn_gamma": nrm((L, 2, G), 1.0),
        "hgrn_norm_g": 1.0 + nrm((L, G), 0.05),
        "rwkv_mu": uni((L, GROUP_COLS[1]), 0.0, 1.0),
        "rwkv_w0": nrm((L, 2, G), 0.5),
        "rwkv_w2": nrm((L, 2, RWKV_DECAY_RANK, G), 0.5 * RWKV_DECAY_RANK ** -0.5),
        "rwkv_a0": nrm((L, 2, G), 0.5),
        "rwkv_a2": nrm((L, 2, RWKV_AAA_RANK, G), 0.5 * RWKV_AAA_RANK ** -0.5),
        "rwkv_g2": nrm((L, RWKV_GATE_RANK, G), RWKV_GATE_RANK ** -0.5),
        "rwkv_k_k": 1.0 + nrm((L, G), 0.05),
        "rwkv_k_a": 1.0 + nrm((L, G), 0.05),
        "rwkv_r_k": nrm((L, H, HEAD_DIM), 0.1),
        "rwkv_ln_g": 1.0 + nrm((L, G), 0.05),
        "rwkv_ln_b": nrm((L, G), 0.02),
        "gdn_conv": nrm((L, GDN_CONV_W, 3 * G), GDN_CONV_W ** -0.5),
        "gdn_a_log": jnp.log(uni((L, 2, H), 1.0, 16.0)),
        "gdn_dt_bias": dt + jnp.log(-jnp.expm1(-dt)),
        "gdn_norm_g": 1.0 + nrm((L, G), 0.05),
        "mlstm_i_bias": nrm((L, 2, H), 0.1),
        "mlstm_f_bias": 3.0 + nrm((L, 2, H), 0.5),
        "mlstm_norm_g": 1.0 + nrm((L, G), 0.05),
    }


def reference(x, c, ctx, c_ctx, ada_w, ada_b, w_in, w_out, ln1_g, ln1_b, ln2_g, ln2_b,
              mlp_w1, mlp_w2, hgrn_gamma, hgrn_norm_g, rwkv_mu, rwkv_w0, rwkv_w2, rwkv_a0,
              rwkv_a2, rwkv_g2, rwkv_k_k, rwkv_k_a, rwkv_r_k, rwkv_ln_g, rwkv_ln_b, gdn_conv,
              gdn_a_log, gdn_dt_bias, gdn_norm_g, mlstm_i_bias, mlstm_f_bias, mlstm_norm_g):
    for layer in range(DEPTH):
        last = layer == DEPTH - 1
        mod_lat = jnp.split((jax.nn.silu(c) @ ada_w[layer] + ada_b[layer])[:, None, :], 6, axis=-1)
        mod_ctx = jnp.split((jax.nn.silu(c_ctx) @ ada_w[layer] + ada_b[layer])[None, None, :], 6, axis=-1)
        h_lat = modulate(x, mod_lat[0], mod_lat[1])
        h_ctx = modulate(ctx, mod_ctx[0], mod_ctx[1])
        p_ctx = split_cols(h_ctx @ w_in[layer], GROUP_COLS)
        p_lat = split_cols(h_lat @ w_in[layer], GROUP_COLS)
        odd = layer % 2 == 1
        col_major = (odd, odd, not odd, not odd)
        p_lat = [to_col_major(t) if cm else t for t, cm in zip(p_lat, col_major)]
        outs = [
            hgrn2_mixer(p_ctx[0], p_lat[0], hgrn_gamma, layer, hgrn_norm_g[layer]),
            rwkv7_mixer(p_ctx[1], p_lat[1], rwkv_mu[layer], rwkv_w0[layer], rwkv_w2[layer],
                        rwkv_a0[layer], rwkv_a2[layer], rwkv_g2[layer], rwkv_k_k[layer],
                        rwkv_k_a[layer], rwkv_r_k[layer], rwkv_ln_g[layer], rwkv_ln_b[layer]),
            gated_deltanet_mixer(p_ctx[2], p_lat[2], gdn_conv[layer], gdn_a_log[layer],
                                 gdn_dt_bias[layer], gdn_norm_g[layer]),
            mlstm_mixer(p_ctx[3], p_lat[3], mlstm_i_bias[layer], mlstm_f_bias[layer],
                        mlstm_norm_g[layer]),
        ]
        y_lat = jnp.concatenate([from_col_major(o[1]) if cm else o[1] for o, cm in zip(outs, col_major)], axis=-1)
        y_lat = y_lat.astype(x.dtype) @ w_out[layer]
        x = layer_norm(DEEPNORM_ALPHA * x + mod_lat[2] * y_lat, ln1_g[layer], ln1_b[layer])
        y_mlp = squared_relu_mlp(modulate(x, mod_lat[3], mod_lat[4]), mlp_w1[layer], mlp_w2[layer])
        x = layer_norm(DEEPNORM_ALPHA * x + mod_lat[5] * y_mlp, ln2_g[layer], ln2_b[layer])
        if not last:
            y_ctx = jnp.concatenate([o[0] for o in outs], axis=-1).astype(ctx.dtype) @ w_out[layer]
            ctx = layer_norm(DEEPNORM_ALPHA * ctx + mod_ctx[2] * y_ctx, ln1_g[layer], ln1_b[layer])
            y_mlp_ctx = squared_relu_mlp(modulate(ctx, mod_ctx[3], mod_ctx[4]), mlp_w1[layer], mlp_w2[layer])
            ctx = layer_norm(DEEPNORM_ALPHA * ctx + mod_ctx[5] * y_mlp_ctx, ln2_g[layer], ln2_b[layer])
    return x
```

```python
import functools
import math

import jax
import jax.numpy as jnp
from jax import lax
from jax.experimental import pallas as pl
from jax.experimental.pallas import tpu as pltpu

f32 = jnp.float32
bf16 = jnp.bfloat16

D_MODEL = 1024
DEPTH = 4
GRID_W = 64
N_GROUPS = 4
GROUP_W = 256
HEAD_DIM = 64
GROUP_HEADS = 4
D_FF = 4 * D_MODEL
CHUNK = 64
SUB = 16
HALO = 8
RWKV_DECAY_RANK = 64
RWKV_AAA_RANK = 64
RWKV_GATE_RANK = 128
GDN_CONV_W = 5
LN_EPS = 1e-5
RWKV_GN_EPS = 64e-5
NORM_EPS = 1e-6
DEEPNORM_ALPHA = (2.0 * DEPTH) ** 0.25
SCALAR_W = 128
W_A = 5 * GROUP_W
W_B = 3 * GROUP_W + 2 * RWKV_DECAY_RANK + 2 * RWKV_AAA_RANK + RWKV_GATE_RANK
W_CD = 4 * GROUP_W + SCALAR_W
VMEM_LIMIT = 56 * 1024 * 1024

NN = (((1,), (0,)), ((), ()))
NT = (((1,), (1,)), ((), ()))
TN = (((0,), (0,)), ((), ()))

G, H, HD, C = GROUP_W, GROUP_HEADS, HEAD_DIM, CHUNK


def _dot(a, b, dims=NN):
    return lax.dot_general(a.astype(bf16), b.astype(bf16), dims, preferred_element_type=f32)


def _hi_lo(x):
    hi = x.astype(bf16)
    lo = (x - hi.astype(f32)).astype(bf16)
    return hi, lo


def _dot3(a, b, dims=NN):
    ah, al = _hi_lo(a)
    bh, bl = _hi_lo(b)
    g = lambda p, q: lax.dot_general(p, q, dims, preferred_element_type=f32)
    return g(ah, bh) + (g(ah, bl) + g(al, bh))


def _dot_exact_lhs(m, x):
    mb = m.astype(bf16)
    h = x.astype(bf16)
    r = x - h.astype(f32)
    mid = r.astype(bf16)
    lo = (r - mid.astype(f32)).astype(bf16)
    g = lambda q: lax.dot_general(mb, q, NN, preferred_element_type=f32)
    return g(h) + (g(mid) + g(lo))


def _softplus(x):
    return jnp.maximum(x, 0.0) + jnp.log1p(jnp.exp(-jnp.abs(x)))


def _log_sigmoid(x):
    return -_softplus(-x)


def _sigmoid(x):
    return jax.nn.sigmoid(x)


def _iota2(n):
    return (lax.broadcasted_iota(jnp.int32, (n, n), 0), lax.broadcasted_iota(jnp.int32, (n, n), 1))


def _before(d, strict, n=C):
    r, c = _iota2(n)
    if d == 0:
        return (c < r) if strict else (c <= r)
    return (c > r) if strict else (c >= r)


def _last_row(t, d):
    n = t.shape[0]
    return t[n - 1:n] if d == 0 else t[0:1]


def _heads3(t):
    return t.reshape(t.shape[0], H, HD)


def _head_l2(t):
    t3 = _heads3(t)
    return (t3 * lax.rsqrt(jnp.sum(t3 * t3, -1, keepdims=True) + 1e-12)).reshape(t.shape)


def _head_rms(t, g):
    t3 = _heads3(t)
    t3 = t3 * lax.rsqrt(jnp.mean(t3 * t3, -1, keepdims=True) + NORM_EPS)
    return t3.reshape(t.shape) * g


def _tri_inverse(L, d):
    r, c = _iota2(C)
    eye = (r == c).astype(f32)

    def same_block(s):
        sh = int(math.log2(s))
        return jnp.right_shift(r, sh) == jnp.right_shift(c, sh)

    p = -jnp.where(same_block(8), L, 0.0)
    x = eye + p
    for _ in range(2):
        p = _dot3(p, p)
        x = x + _dot3(x, p)
    for s in (8, 16, 32):
        ls = jnp.where(jnp.logical_and(same_block(2 * s), jnp.logical_not(same_block(s))), L, 0.0)
        x = x - _dot3(_dot3(x, ls), x)
    return x


def _layer_norm_rows(t, g, b):
    mu = jnp.mean(t, -1, keepdims=True)
    tc = t - mu
    var = jnp.mean(tc * tc, -1, keepdims=True)
    return tc * lax.rsqrt(var + LN_EPS) * g + b


def _hgrn_body(d, x, prev8, next8, of, first, lb_ref, ng_ref, s_ref):
    del prev8, next8

    @pl.when(first)
    def _():
        s_ref[...] = jnp.zeros(s_ref.shape, f32)

    q = x[:, 0:G]
    v = x[:, G:2 * G]
    f = x[:, (2 + d) * G:(3 + d) * G]
    g = x[:, 4 * G:5 * G]
    lb = lb_ref[d:d + 1, :]
    a = jnp.log(lb)
    bb = jnp.log1p(-lb) + _log_sigmoid(f)
    log_f = jnp.maximum(a, bb) + jnp.log1p(jnp.exp(-jnp.abs(a - bb)))
    k = (1.0 - lb) * _sigmoid(-f)

    m16 = _before(d, False, SUB).astype(f32)
    ri = lax.broadcasted_iota(jnp.int32, (SUB, G), 0)
    outs = [None] * (C // SUB)
    for sbi in range(C // SUB):
        sb = sbi if d == 0 else C // SUB - 1 - sbi
        rows = slice(SUB * sb, SUB * sb + SUB)
        lf_s, q_s, k_s, v_s = log_f[rows], q[rows], k[rows], v[rows]
        cum = _dot_exact_lhs(m16, lf_s)
        tot = _last_row(cum, d)
        qe = q_s * jnp.exp(cum)
        ke = k_s * jnp.exp(tot - cum)
        od = jnp.zeros((SUB, H, HD), f32)
        for j in range(SUB):
            e = jnp.exp(jnp.minimum(cum - cum[j:j + 1], 0.0))
            valid = (ri >= j) if d == 0 else (ri <= j)
            t3 = _heads3(jnp.where(valid, q_s * e * k_s[j:j + 1], 0.0))
            sj = jnp.sum(t3, -1, keepdims=True)
            od = od + sj * _heads3(v_s[j:j + 1])
        o_heads = []
        for h in range(H):
            hs = slice(HD * h, HD * h + HD)
            sh = s_ref[h]
            o_heads.append(_dot(qe[:, hs], sh, NT))
            s_ref[h] = sh * jnp.exp(tot[:, hs]) + _dot(v_s[:, hs], ke[:, hs], TN)
        outs[sb] = jnp.concatenate(o_heads, axis=1) + od.reshape(SUB, G)
    o = jnp.concatenate(outs, axis=0)
    if d == 0:
        return o
    return _head_rms(o + of, ng_ref[...]) * (g * _sigmoid(g))


def _gdn_body(d, x, prev8, next8, of, first, cw_ref, alog_ref, dtb_ref, ng_ref, s_ref, ext_ref):
    @pl.when(first)
    def _():
        s_ref[...] = jnp.zeros(s_ref.shape, f32)

    wq = 3 * G
    ext_ref[0:HALO, :] = prev8[:, :wq]
    ext_ref[HALO:HALO + C, :] = x[:, :wq]
    ext_ref[HALO + C:2 * HALO + C, :] = next8[:, :wq]
    acc = jnp.zeros((C, wq), f32)
    for kk in range(GDN_CONV_W):
        acc = acc + cw_ref[kk:kk + 1, :] * ext_ref[pl.ds(HALO - GDN_CONV_W // 2 + kk, C), :]
    qkv = acc * _sigmoid(acc)
    q = _head_l2(qkv[:, 0:G]) * HD ** -0.5
    k = _head_l2(qkv[:, G:2 * G])
    v = qkv[:, 2 * G:3 * G]
    g = x[:, 3 * G:4 * G]
    sc = x[:, 4 * G:4 * G + SCALAR_W]
    beta_all = _sigmoid(sc)
    la_all = -jnp.exp(alog_ref[...]) * _softplus(sc + dtb_ref[...])
    incl = _before(d, False)
    strict = _before(d, True)
    cum_all = _dot_exact_lhs(incl.astype(f32), la_all)
    cum_rows = cum_all.T
    last_all = _last_row(cum_all, d)
    o_heads = []
    for h in range(H):
        hs = slice(HD * h, HD * h + HD)
        lb_, la_ = 4 * d + h, 8 + 4 * d + h
        qh, kh, vh = q[:, hs], k[:, hs], v[:, hs]
        bc = beta_all[:, lb_:lb_ + 1]
        cc = cum_all[:, la_:la_ + 1]
        cr = cum_rows[la_:la_ + 1, :]
        last = last_all[:, la_:la_ + 1]
        gam = jnp.where(incl, jnp.exp(jnp.minimum(cc - cr, 0.0)), 0.0)
        kb = kh * bc
        L = jnp.where(strict, _dot3(kb, kh, NT) * gam, 0.0)
        T = _tri_inverse(L, d)
        ec = jnp.exp(cc)
        sol = _dot3(T, jnp.concatenate([vh * bc, kb * ec], axis=1))
        u, w = sol[:, :HD], sol[:, HD:]
        s = s_ref[h]
        vn = u - _dot3(w, s)
        attn = _dot(qh, kh, NT) * gam
        o_heads.append(_dot(qh * ec, s) + _dot(attn, vn))
        s_ref[h] = jnp.exp(last) * s + _dot3(kh * jnp.exp(last - cc), vn, TN)
    o = jnp.concatenate(o_heads, axis=1)
    if d == 0:
        return o
    return _head_rms(o + of, ng_ref[...]) * (g * _sigmoid(g))


def _mlstm_body(d, x, prev8, next8, of, first, ib_ref, fb_ref, ng_ref, c_ref, m_ref):
    del prev8, next8

    @pl.when(first)
    def _():
        c_ref[...] = jnp.zeros(c_ref.shape, f32)
        m_ref[...] = jnp.zeros(m_ref.shape, f32)

    q = x[:, 0:G]
    k = x[:, G:2 * G] * HD ** -0.5
    v = x[:, 2 * G:3 * G]
    og = x[:, 3 * G:4 * G]
    sc = x[:, 4 * G:4 * G + SCALAR_W]
    li_all = sc + ib_ref[...]
    lf_all = _log_sigmoid(sc + fb_ref[...])
    incl = _before(d, False)
    b_all = _dot_exact_lhs(incl.astype(f32), lf_all)
    b_rows = b_all.T
    li_rows = li_all.T
    bl_all = _last_row(b_all, d)
    lane = lax.broadcasted_iota(jnp.int32, (C, HD), 1)
    ones_col = (lane == 0).astype(f32)
    o_heads = []
    for h in range(H):
        hs = slice(HD * h, HD * h + HD)
        li_, lf_ = 4 * d + h, 8 + 4 * d + h
        qh, kh = q[:, hs], k[:, hs]
        vp = jnp.concatenate([v[:, hs], ones_col], axis=1)
        bcol = b_all[:, lf_:lf_ + 1]
        brow = b_rows[lf_:lf_ + 1, :]
        licol = li_all[:, li_:li_ + 1]
        lirow = li_rows[li_:li_ + 1, :]
        bl = bl_all[:, lf_:lf_ + 1]
        m = m_ref[0:1, h:h + 1]
        dlog = jnp.where(incl, bcol - brow + lirow, -jnp.inf)
        inter = bcol + m
        m_i = jnp.maximum(inter, jnp.max(dlog, -1, keepdims=True))
        dw = jnp.exp(dlog - m_i)
        iw = jnp.exp(inter - m_i)
        s_ = _dot(qh, kh, NT) * dw
        cst = c_ref[h]
        num = iw * _dot(qh, cst) + _dot(s_, vp)
        den = num[:, HD:HD + 1]
        o_heads.append(num[:, :HD] / jnp.maximum(jnp.abs(den), jnp.exp(-m_i)))
        src = bl - bcol + licol
        m_new = jnp.maximum(bl + m, jnp.max(src, 0, keepdims=True))
        c_ref[h] = jnp.exp(bl + m - m_new) * cst + _dot(kh * jnp.exp(src - m_new), vp, TN)
        m_ref[0:1, h:h + 1] = m_new
    o = jnp.concatenate(o_heads, axis=1)
    if d == 0:
        return o
    return _head_rms(o + of, ng_ref[...]) * _sigmoid(og)


def _rwkv_body(d, x, prev8, next8, of, first, mu_ref, w0_ref, w2_ref, a0_ref, a2_ref, g2_ref,
               kk_ref, ka_ref, rk_ref, lng_ref, lnb_ref, s_ref, ext_ref):
    @pl.when(first)
    def _():
        s_ref[...] = jnp.zeros(s_ref.shape, f32)

    ext_ref[0:HALO, :] = prev8
    ext_ref[HALO:HALO + C, :] = x
    ext_ref[HALO + C:2 * HALO + C, :] = next8
    nb = ext_ref[pl.ds(HALO - 1, C), :] + ext_ref[pl.ds(HALO + 1, C), :]
    xs = x + mu_ref[...] * (0.5 * nb - x)
    r = xs[:, 0:G]
    k = xs[:, G:2 * G]
    v = xs[:, 2 * G:3 * G]
    o_w = 3 * G
    o_a = o_w + 2 * RWKV_DECAY_RANK
    o_g = o_a + 2 * RWKV_AAA_RANK

    def a_of(dd):
        ad = xs[:, o_a + RWKV_AAA_RANK * dd:o_a + RWKV_AAA_RANK * (dd + 1)]
        return _sigmoid(a0_ref[dd:dd + 1, :] + _dot3(ad, a2_ref[dd]))

    wd = xs[:, o_w + RWKV_DECAY_RANK * d:o_w + RWKV_DECAY_RANK * (d + 1)]
    w = -_softplus(-(w0_ref[d:d + 1, :] + _dot3(jnp.tanh(wd), w2_ref[d]))) - 0.5
    logw = -jnp.exp(w)
    a = a_of(d)
    kk = _head_l2(k * kk_ref[...])
    kd = k * (1.0 + (a - 1.0) * ka_ref[...])
    beta = kk * a
    incl = _before(d, False)
    strict = _before(d, True)
    lc = _dot_exact_lhs(incl.astype(f32), logw)
    tot = _last_row(lc, d)
    e_neg = jnp.exp(-lc)
    e_tot = jnp.exp(tot - lc)
    kp = kk * jnp.exp(lc - logw)
    rp = r * jnp.exp(lc)
    bn = beta * e_neg
    kn = kd * e_neg
    ke = kd * e_tot
    be = beta * e_tot
    o_heads = []
    for h in range(H):
        hs = slice(HD * h, HD * h + HD)
        sh = s_ref[h]
        vh = v[:, hs]
        a_ab = jnp.where(strict, _dot3(kp[:, hs], bn[:, hs], NT), 0.0)
        a_ak = jnp.where(strict, _dot3(kp[:, hs], kn[:, hs], NT), 0.0)
        a_rb = jnp.where(incl, _dot(rp[:, hs], bn[:, hs], NT), 0.0)
        a_rk = jnp.where(incl, _dot(rp[:, hs], kn[:, hs], NT), 0.0)
        T = _tri_inverse(a_ab, d)
        u = _dot3(T, _dot3(kp[:, hs], sh, NT) + _dot3(a_ak, vh))
        o_heads.append(_dot(rp[:, hs], sh, NT) - _dot(a_rb, u) + _dot(a_rk, vh))
        s_ref[h] = sh * jnp.exp(tot[:, hs]) + _dot3(vh, ke[:, hs], TN) - _dot3(u, be[:, hs], TN)
    o = jnp.concatenate(o_heads, axis=1)
    if d == 0:
        return o
    kd_other = k * (1.0 + (a_of(1 - d) - 1.0) * ka_ref[...])
    s_both = jnp.sum(_heads3(r * (kd + kd_other) * rk_ref[...]), -1, keepdims=True)
    bonus = (s_both * _heads3(v)).reshape(C, G)
    t3 = _heads3(o + of)
    mu = jnp.mean(t3, -1, keepdims=True)
    tc = t3 - mu
    var = jnp.mean(tc * tc, -1, keepdims=True)
    gn = (tc * lax.rsqrt(var + RWKV_GN_EPS)).reshape(C, G) * lng_ref[...] + lnb_ref[...]
    gate = _dot(_sigmoid(xs[:, o_g:o_g + RWKV_GATE_RANK]), g2_ref[...])
    return (gn + bonus) * gate


def _mixer_call(body, d, colmajor, use_halo, p_ctx, p_lat, of_ctx, of_lat, params, scratch, name):
    B, n_ctx, W = p_ctx.shape
    n_lat = p_lat.shape[1]
    nc, nl = n_ctx // C, n_lat // C
    if colmajor:
        assert n_lat == GRID_W * C
    have_of = d == 1

    def cj(s):
        return jnp.clip(s if d == 0 else nc - 1 - s, 0, nc - 1)

    def lj(s):
        return jnp.clip((s - nc) if d == 0 else nl - 1 - (s - nc), 0, nl - 1)

    def lat_view(t, w):
        return t.reshape(B, C, GRID_W * w) if colmajor else t

    def lat_spec(w):
        if colmajor:
            return pl.BlockSpec((1, C, w), lambda b, s: (b, 0, lj(s)))
        return pl.BlockSpec((1, C, w), lambda b, s: (b, lj(s), 0))

    def ctx_spec(w):
        return pl.BlockSpec((1, C, w), lambda b, s: (b, cj(s), 0))

    rpc = C // HALO
    operands = [p_ctx, lat_view(p_lat, W)]
    in_specs = [ctx_spec(W), lat_spec(W)]
    if use_halo:
        operands += [p_ctx, p_ctx, lat_view(p_lat, W), lat_view(p_lat, W)]
        in_specs += [
            pl.BlockSpec((1, HALO, W), lambda b, s: (b, jnp.maximum(rpc * cj(s) - 1, 0), 0)),
            pl.BlockSpec((1, HALO, W), lambda b, s: (b, jnp.minimum(rpc * (cj(s) + 1), rpc * nc - 1), 0)),
        ]
        if colmajor:
            in_specs += [
                pl.BlockSpec((1, HALO, W), lambda b, s: (b, rpc - 1, jnp.maximum(lj(s) - 1, 0))),
                pl.BlockSpec((1, HALO, W), lambda b, s: (b, 0, jnp.minimum(lj(s) + 1, nl - 1))),
            ]
        else:
            in_specs += [
                pl.BlockSpec((1, HALO, W), lambda b, s: (b, jnp.maximum(rpc * lj(s) - 1, 0), 0)),
                pl.BlockSpec((1, HALO, W), lambda b, s: (b, jnp.minimum(rpc * (lj(s) + 1), rpc * nl - 1), 0)),
            ]
    if have_of:
        operands += [of_ctx, lat_view(of_lat, G)]
        in_specs += [ctx_spec(G), lat_spec(G)]
    n_data = len(operands)
    for p in params:
        operands.append(p)
        in_specs.append(pl.BlockSpec(p.shape, lambda b, s, _n=p.ndim: (0,) * _n))
    n_par = len(params)

    def kern(*refs):
        data, par = refs[:n_data], refs[n_data:n_data + n_par]
        oc_ref, ol_ref = refs[n_data + n_par:n_data + n_par + 2]
        scr = refs[n_data + n_par + 2:]
        s = pl.program_id(1)
        is_ctx = s < nc
        x = jnp.where(is_ctx, data[0][0], data[1][0])
        i = 2
        prev8 = next8 = None
        if use_halo:
            j = jnp.where(is_ctx, cj(s), lj(s))
            jmax = jnp.where(is_ctx, nc - 1, nl - 1)
            prev8 = jnp.where(j > 0, jnp.where(is_ctx, data[2][0], data[4][0]), 0.0)
            next8 = jnp.where(j < jmax, jnp.where(is_ctx, data[3][0], data[5][0]), 0.0)
            i = 6
        of = jnp.where(is_ctx, data[i][0], data[i + 1][0]) if have_of else None
        y = body(d, x, prev8, next8, of, s == 0, *par, *scr)

        @pl.when(is_ctx)
        def _():
            oc_ref[0] = y

        @pl.when(jnp.logical_not(is_ctx))
        def _():
            ol_ref[0] = y

    out_ctx, out_lat = pl.pallas_call(
        kern,
        grid=(B, nc + nl),
        in_specs=in_specs,
        out_specs=[ctx_spec(G), lat_spec(G)],
        out_shape=[jax.ShapeDtypeStruct((B, n_ctx, G), f32),
                   jax.ShapeDtypeStruct((B, C, GRID_W * G) if colmajor else (B, n_lat, G), f32)],
        scratch_shapes=scratch,
        compiler_params=pltpu.CompilerParams(dimension_semantics=("arbitrary", "arbitrary"),
                                             vmem_limit_bytes=VMEM_LIMIT),
        name=name,
    )(*operands)
    return out_ctx, out_lat.reshape(B, n_lat, G)


def _run_mixer(body, colmajor, use_halo, p_ctx, p_lat, params, scratch, name):
    f_ctx, f_lat = _mixer_call(body, 0, colmajor, use_halo, p_ctx, p_lat, None, None, params, scratch, name + "_fwd")
    return _mixer_call(body, 1, colmajor, use_halo, p_ctx, p_lat, f_ctx, f_lat, params, scratch, name + "_bwd")


def _ada_call(cc, ada_w, ada_b):
    L, _, n6 = ada_w.shape
    tn = 1536

    def kern(c_ref, w_ref, b_ref, o_ref):
        cv = c_ref[...]
        o_ref[0] = _dot3(cv * _sigmoid(cv), w_ref[0]) + b_ref[0]

    return pl.pallas_call(
        kern,
        grid=(L, n6 // tn),
        in_specs=[pl.BlockSpec(cc.shape, lambda l, j: (0, 0)),
                  pl.BlockSpec((1, D_MODEL, tn), lambda l, j: (l, 0, j)),
                  pl.BlockSpec((1, 1, tn), lambda l, j: (l, 0, j))],
        out_specs=pl.BlockSpec((1, cc.shape[0], tn), lambda l, j: (l, 0, j)),
        out_shape=jax.ShapeDtypeStruct((L, cc.shape[0], n6), f32),
        compiler_params=pltpu.CompilerParams(dimension_semantics=("arbitrary", "arbitrary"),
                                             vmem_limit_bytes=VMEM_LIMIT),
        name="ada",
    )(cc, ada_w, ada_b.reshape(L, 1, n6))


def _mod_spec(per_batch, k):
    if per_batch:
        return pl.BlockSpec((1, 1, D_MODEL), lambda b, i: (b, 0, k))
    return pl.BlockSpec((1, 1, D_MODEL), lambda b, i: (0, 0, k))


def _const_spec(a):
    return pl.BlockSpec(a.shape, lambda b, i, _n=a.ndim: (0,) * _n)


def _proj_call(xin, mod, weights, per_batch, tm, name):
    B, n, _ = xin.shape
    widths = [w.shape[1] for w in weights]

    def kern(x_ref, sh_ref, sc_ref, *rest):
        w_refs, o_refs = rest[:len(weights)], rest[len(weights):]
        hm = (x_ref[0] * (1.0 + sc_ref[0]) + sh_ref[0]).astype(bf16)
        for w_ref, o_ref in zip(w_refs, o_refs):
            o_ref[0] = jnp.dot(hm, w_ref[...], preferred_element_type=f32)

    return pl.pallas_call(
        kern,
        grid=(B, n // tm),
        in_specs=[pl.BlockSpec((1, tm, D_MODEL), lambda b, i: (b, i, 0)),
                  _mod_spec(per_batch, 0), _mod_spec(per_batch, 1)] + [_const_spec(w) for w in weights],
        out_specs=[pl.BlockSpec((1, tm, w), lambda b, i: (b, i, 0)) for w in widths],
        out_shape=[jax.ShapeDtypeStruct((B, n, w), f32) for w in widths],
        compiler_params=pltpu.CompilerParams(dimension_semantics=("arbitrary", "arbitrary"),
                                             vmem_limit_bytes=VMEM_LIMIT),
        name=name,
    )(xin, mod, mod, *weights)


def _outproj_call(xin, ys, mod, w_out, ln_g, ln_b, per_batch, tm, name):
    B, n, _ = xin.shape

    def kern(x_ref, ya, yb, yc, yd, g_ref, w_ref, lg_ref, lb_ref, o_ref):
        acc = jnp.zeros((tm, D_MODEL), f32)
        for gi, y_ref in enumerate((ya, yb, yc, yd)):
            acc = acc + jnp.dot(y_ref[0].astype(bf16), w_ref[gi * G:(gi + 1) * G, :], preferred_element_type=f32)
        o_ref[0] = _layer_norm_rows(DEEPNORM_ALPHA * x_ref[0] + g_ref[0] * acc, lg_ref[...], lb_ref[...])

    yspec = pl.BlockSpec((1, tm, G), lambda b, i: (b, i, 0))
    return pl.pallas_call(
        kern,
        grid=(B, n // tm),
        in_specs=[pl.BlockSpec((1, tm, D_MODEL), lambda b, i: (b, i, 0)), yspec, yspec, yspec, yspec,
                  _mod_spec(per_batch, 2), _const_spec(w_out), _const_spec(ln_g), _const_spec(ln_b)],
        out_specs=pl.BlockSpec((1, tm, D_MODEL), lambda b, i: (b, i, 0)),
        out_shape=jax.ShapeDtypeStruct((B, n, D_MODEL), f32),
        compiler_params=pltpu.CompilerParams(dimension_semantics=("arbitrary", "arbitrary"),
                                             vmem_limit_bytes=VMEM_LIMIT),
        name=name,
    )(xin, *ys, mod, w_out, ln_g, ln_b)


def _mlp_call(xin, mod, w1, w2, ln_g, ln_b, per_batch, tm, name):
    B, n, _ = xin.shape
    tf = 1024

    def kern(x_ref, sh_ref, sc_ref, g_ref, w1_ref, w2_ref, lg_ref, lb_ref, o_ref):
        xv = x_ref[0]
        hm = (xv * (1.0 + sc_ref[0]) + sh_ref[0]).astype(bf16)
        acc = jnp.zeros((tm, D_MODEL), f32)
        for j in range(D_FF // tf):
            a = jnp.maximum(jnp.dot(hm, w1_ref[:, j * tf:(j + 1) * tf], preferred_element_type=f32), 0.0)
            acc = acc + jnp.dot((a * a).astype(bf16), w2_ref[j * tf:(j + 1) * tf, :], preferred_element_type=f32)
        o_ref[0] = _layer_norm_rows(DEEPNORM_ALPHA * xv + g_ref[0] * acc, lg_ref[...], lb_ref[...])

    return pl.pallas_call(
        kern,
        grid=(B, n // tm),
        in_specs=[pl.BlockSpec((1, tm, D_MODEL), lambda b, i: (b, i, 0)),
                  _mod_spec(per_batch, 3), _mod_spec(per_batch, 4), _mod_spec(per_batch, 5),
                  _const_spec(w1), _const_spec(w2), _const_spec(ln_g), _const_spec(ln_b)],
        out_specs=pl.BlockSpec((1, tm, D_MODEL), lambda b, i: (b, i, 0)),
        out_shape=jax.ShapeDtypeStruct((B, n, D_MODEL), f32),
        compiler_params=pltpu.CompilerParams(dimension_semantics=("arbitrary", "arbitrary"),
                                             vmem_limit_bytes=VMEM_LIMIT),
        name=name,
    )(xin, mod, mod, mod, w1, w2, ln_g, ln_b)


def _pad_lanes(t, start, width=SCALAR_W):
    flat = t.reshape(1, -1).astype(f32)
    return jnp.pad(flat, ((0, 0), (start, width - start - flat.shape[1])))


def _split_w_in(w):
    oa, ob, oc = W_A, W_A + W_B, W_A + W_B + (3 * G + 4 * H + G)
    wa, wb, wc, wd = w[:, :oa], w[:, oa:ob], w[:, ob:oc], w[:, oc:]

    def relayout(t):
        qkv, scal, gate = t[:, :3 * G], t[:, 3 * G:3 * G + 4 * H], t[:, 3 * G + 4 * H:]
        return jnp.concatenate([qkv, gate, scal, jnp.zeros((t.shape[0], SCALAR_W - 4 * H), t.dtype)], axis=1)

    return [t.astype(bf16) for t in (wa, wb, relayout(wc), relayout(wd))]


def kernel(x, c, ctx, c_ctx, ada_w, ada_b, w_in, w_out, ln1_g, ln1_b, ln2_g, ln2_b, mlp_w1, mlp_w2,
           hgrn_gamma, hgrn_norm_g, rwkv_mu, rwkv_w0, rwkv_w2, rwkv_a0, rwkv_a2, rwkv_g2, rwkv_k_k,
           rwkv_k_a, rwkv_r_k, rwkv_ln_g, rwkv_ln_b, gdn_conv, gdn_a_log, gdn_dt_bias, gdn_norm_g,
           mlstm_i_bias, mlstm_f_bias, mlstm_norm_g):
    B = x.shape[0]
    depth = ada_w.shape[0]
    cc = jnp.concatenate([c, c_ctx[None, :], jnp.zeros((8 - B - 1, D_MODEL), f32)], axis=0)
    mods = _ada_call(cc, ada_w, ada_b)
    lb_cum = jnp.cumsum(jax.nn.softmax(hgrn_gamma.astype(f32), axis=0), axis=0)
    row = lambda t: t.reshape(1, -1).astype(f32)
    state = lambda w: pltpu.VMEM((H, HD, w), f32)

    for layer in range(depth):
        last = layer == depth - 1
        odd = layer % 2 == 1
        mod_lat = mods[layer, :B].reshape(B, 1, 6 * D_MODEL)
        mod_ctx = mods[layer, B:B + 1].reshape(1, 1, 6 * D_MODEL)
        w_groups = _split_w_in(w_in[layer])
        p_lat = _proj_call(x, mod_lat, w_groups, True, 256, "proj_lat")
        p_ctx = _proj_call(ctx, mod_ctx, w_groups, False, 256, "proj_ctx")

        lb = lb_cum[layer] - lb_cum[0]
        ys = [
            _run_mixer(_hgrn_body, odd, False, p_ctx[0], p_lat[0],
                       [lb, row(hgrn_norm_g[layer])], [state(HD)], "hgrn"),
            _run_mixer(_rwkv_body, odd, True, p_ctx[1], p_lat[1],
                       [row(rwkv_mu[layer]), rwkv_w0[layer], rwkv_w2[layer], rwkv_a0[layer], rwkv_a2[layer],
                        rwkv_g2[layer], row(rwkv_k_k[layer]), row(rwkv_k_a[layer]), row(rwkv_r_k[layer]),
                        row(rwkv_ln_g[layer]), row(rwkv_ln_b[layer])],
                       [state(HD), pltpu.VMEM((C + 2 * HALO, W_B), f32)], "rwkv"),
            _run_mixer(_gdn_body, not odd, True, p_ctx[2], p_lat[2],
                       [gdn_conv[layer], _pad_lanes(gdn_a_log[layer], 2 * H), _pad_lanes(gdn_dt_bias[layer], 2 * H),
                        row(gdn_norm_g[layer])],
                       [state(HD), pltpu.VMEM((C + 2 * HALO, 3 * G), f32)], "gdn"),
            _run_mixer(_mlstm_body, not odd, False, p_ctx[3], p_lat[3],
                       [_pad_lanes(mlstm_i_bias[layer], 0), _pad_lanes(mlstm_f_bias[layer], 2 * H),
                        row(mlstm_norm_g[layer])],
                       [state(2 * HD), pltpu.VMEM((8, SCALAR_W), f32)], "mlstm"),
        ]
        wo = w_out[layer].astype(bf16)
        w1 = mlp_w1[layer].astype(bf16)
        w2 = mlp_w2[layer].astype(bf16)
        g1, b1, g2, b2 = row(ln1_g[layer]), row(ln1_b[layer]), row(ln2_g[layer]), row(ln2_b[layer])
        x = _outproj_call(x, [y[1] for y in ys], mod_lat, wo, g1, b1, True, 256, "outproj_lat")
        x = _mlp_call(x, mod_lat, w1, w2, g2, b2, True, 256, "mlp_lat")
        if not last:
            ctx = _outproj_call(ctx, [y[0] for y in ys], mod_ctx, wo, g1, b1, False, 256, "outproj_ctx")
            ctx = _mlp_call(ctx, mod_ctx, w1, w2, g2, b2, False, 256, "mlp_ctx")
    return x
```

```python
import math

import jax
import jax.numpy as jnp
from jax import lax
from jax.experimental import pallas as pl
from jax.experimental.pallas import tpu as pltpu

f32 = jnp.float32
bf16 = jnp.bfloat16

D_MODEL = 1024
DEPTH = 4
GRID_W = 64
GROUP_W = 256
HEAD_DIM = 64
GROUP_HEADS = 4
D_FF = 4 * D_MODEL
CHUNK = 64
SUB = 16
HALO = 8
RWKV_DECAY_RANK = 64
RWKV_AAA_RANK = 64
RWKV_GATE_RANK = 128
GDN_CONV_W = 5
LN_EPS = 1e-5
RWKV_GN_EPS = 64e-5
NORM_EPS = 1e-6
DEEPNORM_ALPHA = (2.0 * DEPTH) ** 0.25
SCALAR_W = 128
W_A = 5 * GROUP_W
W_B = 3 * GROUP_W + 2 * RWKV_DECAY_RANK + 2 * RWKV_AAA_RANK + RWKV_GATE_RANK
W_CD = 4 * GROUP_W + SCALAR_W
VMEM_LIMIT = 56 * 1024 * 1024

NN = (((1,), (0,)), ((), ()))
NT = (((1,), (1,)), ((), ()))
TN = (((0,), (0,)), ((), ()))

G, H, HD, C = GROUP_W, GROUP_HEADS, HEAD_DIM, CHUNK


def _mm(a, b, dims=NN):
    return lax.dot_general(a, b, dims, preferred_element_type=f32)


def _dot(a, b, dims=NN):
    return _mm(a.astype(bf16), b.astype(bf16), dims)


def _split(x):
    hi = x.astype(bf16)
    lo = (x - hi.astype(f32)).astype(bf16)
    return hi, lo


def _dot3s(a, b, dims=NN):
    return _mm(a[0], b[0], dims) + (_mm(a[0], b[1], dims) + _mm(a[1], b[0], dims))


def _dot3(a, b, dims=NN):
    return _dot3s(_split(a), _split(b), dims)


def _dot_exact_lhs(m, x):
    mb = m.astype(bf16)
    h = x.astype(bf16)
    r = x - h.astype(f32)
    mid = r.astype(bf16)
    lo = (r - mid.astype(f32)).astype(bf16)
    return _mm(mb, h) + (_mm(mb, mid) + _mm(mb, lo))


def _softplus(x):
    return jnp.maximum(x, 0.0) + jnp.log1p(jnp.exp(-jnp.abs(x)))


def _log_sigmoid(x):
    return -_softplus(-x)


def _sigmoid(x):
    return jax.nn.sigmoid(x)


def _iota2(n):
    return (lax.broadcasted_iota(jnp.int32, (n, n), 0), lax.broadcasted_iota(jnp.int32, (n, n), 1))


def _same_block(r, c, size):
    sh = int(math.log2(size))
    return jnp.right_shift(r, sh) == jnp.right_shift(c, sh)


def _before(d, strict, n=C, block=None):
    r, c = _iota2(n)
    if d == 0:
        m = (c < r) if strict else (c <= r)
    else:
        m = (c > r) if strict else (c >= r)
    if block is not None and block != n:
        m = jnp.logical_and(m, _same_block(r, c, block))
    return m


def _heads3(t):
    return t.reshape(t.shape[0], H, HD)


def _head_l2(t):
    t3 = _heads3(t)
    return (t3 * lax.rsqrt(jnp.sum(t3 * t3, -1, keepdims=True) + 1e-12)).reshape(t.shape)


def _head_rms(t, g):
    t3 = _heads3(t)
    t3 = t3 * lax.rsqrt(jnp.mean(t3 * t3, -1, keepdims=True) + NORM_EPS)
    return t3.reshape(t.shape) * g


def _tri_inverse_many(Ls):
    r, c = _iota2(C)
    eye = (r == c).astype(f32)
    blk8 = _same_block(r, c, 8)
    ps = [-jnp.where(blk8, L, 0.0) for L in Ls]
    xs = [eye + p for p in ps]
    for _ in range(2):
        pss = [_split(p) for p in ps]
        ps = [_dot3s(p, p) for p in pss]
        pss = [_split(p) for p in ps]
        xs = [x + _dot3s(_split(x), p) for x, p in zip(xs, pss)]
    for s in (8, 16, 32):
        ring = jnp.logical_and(_same_block(r, c, 2 * s), jnp.logical_not(_same_block(r, c, s)))
        xss = [_split(x) for x in xs]
        ts = [_dot3s(x, _split(jnp.where(ring, L, 0.0))) for x, L in zip(xss, Ls)]
        xs = [x - _dot3s(_split(t), x2) for x, t, x2 in zip(xs, ts, xss)]
    return xs


def _layer_norm_rows(t, g, b):
    mu = jnp.mean(t, -1, keepdims=True)
    tc = t - mu
    var = jnp.mean(tc * tc, -1, keepdims=True)
    return tc * lax.rsqrt(var + LN_EPS) * g + b


def _last_rows(t, d, nb, rows):
    return [t[rows * bb + (rows - 1 if d == 0 else 0):rows * bb + (rows if d == 0 else 1)] for bb in range(nb)]


def _instances(nb):
    return [(bb, h) for bb in range(nb) for h in range(H)]


def _rs(bb):
    return slice(C * bb, C * bb + C)


def _hs(h):
    return slice(HD * h, HD * h + HD)


def _assemble(o_inst, nb):
    return jnp.concatenate([jnp.concatenate(o_inst[bb * H:(bb + 1) * H], axis=1) for bb in range(nb)], axis=0)


def _hgrn_body(d, x, prev8, next8, of, first, lb_ref, ng_ref, s_ref):
    del prev8, next8
    nb = x.shape[0]
    inst = _instances(nb)

    @pl.when(first)
    def _():
        s_ref[...] = jnp.zeros(s_ref.shape, f32)

    xf = x.reshape(nb * C, x.shape[2])
    q = xf[:, 0:G]
    v = xf[:, G:2 * G]
    f = xf[:, (2 + d) * G:(3 + d) * G]
    g = xf[:, 4 * G:5 * G]
    lb = lb_ref[d:d + 1, :]
    a = jnp.log(lb)
    bb_ = jnp.log1p(-lb) + _log_sigmoid(f)
    log_f = jnp.maximum(a, bb_) + jnp.log1p(jnp.exp(-jnp.abs(a - bb_)))
    k = (1.0 - lb) * _sigmoid(-f)

    ns = nb * SUB
    m16 = _before(d, False, ns, SUB).astype(f32)
    lr, lc_ = _iota2(G)
    seg = _same_block(lr, lc_, HD).astype(bf16)
    ri = lax.broadcasted_iota(jnp.int32, (nb, SUB, G), 1)
    n_sub = C // SUB
    o_sub = [None] * n_sub
    for sbi in range(n_sub):
        sb = sbi if d == 0 else n_sub - 1 - sbi
        pick = lambda t: jnp.concatenate([t[C * b + SUB * sb:C * b + SUB * sb + SUB] for b in range(nb)], axis=0)
        lf_s, q_s, k_s, v_s = pick(log_f), pick(q), pick(k), pick(v)
        cum = _dot_exact_lhs(m16, lf_s)
        cum3, q3, k3, v3 = (t.reshape(nb, SUB, G) for t in (cum, q_s, k_s, v_s))
        tot3 = cum3[:, SUB - 1:SUB, :] if d == 0 else cum3[:, 0:1, :]
        qe3 = q3 * jnp.exp(cum3)
        ke3 = k3 * jnp.exp(tot3 - cum3)
        ts = []
        for j in range(SUB):
            e = jnp.exp(jnp.minimum(cum3 - cum3[:, j:j + 1, :], 0.0))
            valid = (ri >= j) if d == 0 else (ri <= j)
            ts.append(jnp.where(valid, q3 * e * k3[:, j:j + 1, :], 0.0).astype(bf16).reshape(ns, G))
        rsum = _mm(jnp.concatenate(ts, axis=0), seg)
        od = jnp.zeros((nb, SUB, G), f32)
        for j in range(SUB):
            od = od + rsum[j * ns:(j + 1) * ns].reshape(nb, SUB, G) * v3[:, j:j + 1, :]
        ss = [s_ref[b * H + h] for b, h in inst]
        oi = [_dot(qe3[b][:, _hs(h)], s, NT) for (b, h), s in zip(inst, ss)]
        for (b, h), s in zip(inst, ss):
            s_ref[b * H + h] = s * jnp.exp(tot3[b][:, _hs(h)]) + _dot(v3[b][:, _hs(h)], ke3[b][:, _hs(h)], TN)
        o_sub[sb] = [jnp.concatenate(oi[b * H:(b + 1) * H], axis=1) + od[b] for b in range(nb)]
    o = jnp.concatenate([o_sub[sb][b] for b in range(nb) for sb in range(n_sub)], axis=0)
    if d == 0:
        return o
    return _head_rms(o + of, ng_ref[...]) * (g * _sigmoid(g))


def _gdn_body(d, x, prev8, next8, of, first, cw_ref, alog_ref, dtb_ref, ng_ref, s_ref, ext_ref):
    nb = x.shape[0]
    inst = _instances(nb)

    @pl.when(first)
    def _():
        s_ref[...] = jnp.zeros(s_ref.shape, f32)

    wq = 3 * G
    accs = []
    for b in range(nb):
        ext_ref[b, 0:HALO, :] = prev8[b][:, :wq]
        ext_ref[b, HALO:HALO + C, :] = x[b][:, :wq]
        ext_ref[b, HALO + C:2 * HALO + C, :] = next8[b][:, :wq]
        acc = jnp.zeros((C, wq), f32)
        for kk in range(GDN_CONV_W):
            acc = acc + cw_ref[kk:kk + 1, :] * ext_ref[b, pl.ds(HALO - GDN_CONV_W // 2 + kk, C), :]
        accs.append(acc)
    acc = jnp.concatenate(accs, axis=0)
    xf = x.reshape(nb * C, x.shape[2])
    qkv = acc * _sigmoid(acc)
    q = _head_l2(qkv[:, 0:G]) * HD ** -0.5
    k = _head_l2(qkv[:, G:2 * G])
    v = qkv[:, 2 * G:3 * G]
    g = xf[:, 3 * G:4 * G]
    sc = xf[:, 4 * G:4 * G + SCALAR_W]
    beta_all = _sigmoid(sc)
    la_all = -jnp.exp(alog_ref[...]) * _softplus(sc + dtb_ref[...])
    incl = _before(d, False)
    strict = _before(d, True)
    cum_all = _dot_exact_lhs(_before(d, False, nb * C, C).astype(f32), la_all)
    cum_rows = cum_all.T
    last_all = _last_rows(cum_all, d, nb, C)
    lb_, la_ = 4 * d, 8 + 4 * d
    qh = [q[_rs(b), _hs(h)] for b, h in inst]
    kh = [k[_rs(b), _hs(h)] for b, h in inst]
    vh = [v[_rs(b), _hs(h)] for b, h in inst]
    bc = [beta_all[_rs(b), lb_ + h:lb_ + h + 1] for b, h in inst]
    cc = [cum_all[_rs(b), la_ + h:la_ + h + 1] for b, h in inst]
    cr = [cum_rows[la_ + h:la_ + h + 1, _rs(b)] for b, h in inst]
    last = [last_all[b][:, la_ + h:la_ + h + 1] for b, h in inst]
    gam = [jnp.where(incl, jnp.exp(jnp.minimum(c_ - r_, 0.0)), 0.0) for c_, r_ in zip(cc, cr)]
    kb = [k_ * b_ for k_, b_ in zip(kh, bc)]
    Ls = [jnp.where(strict, _dot3(kb_, k_, NT) * g_, 0.0) for kb_, k_, g_ in zip(kb, kh, gam)]
    Ts = _tri_inverse_many(Ls)
    ec = [jnp.exp(c_) for c_ in cc]
    sol = [_dot3(T, jnp.concatenate([v_ * b_, kb_ * e_], axis=1))
           for T, v_, b_, kb_, e_ in zip(Ts, vh, bc, kb, ec)]
    ss = [s_ref[b * H + h] for b, h in inst]
    vn = [so[:, :HD] - _dot3(so[:, HD:], s) for so, s in zip(sol, ss)]
    attn = [_dot(q_, k_, NT) * g_ for q_, k_, g_ in zip(qh, kh, gam)]
    o_inst = [_dot(q_ * e_, s) + _dot(a_, vn_) for q_, e_, s, a_, vn_ in zip(qh, ec, ss, attn, vn)]
    for (b, h), s, l_, k_, c_, vn_ in zip(inst, ss, last, kh, cc, vn):
        s_ref[b * H + h] = jnp.exp(l_) * s + _dot3(k_ * jnp.exp(l_ - c_), vn_, TN)
    o = _assemble(o_inst, nb)
    if d == 0:
        return o
    return _head_rms(o + of, ng_ref[...]) * (g * _sigmoid(g))


def _mlstm_body(d, x, prev8, next8, of, first, ib_ref, fb_ref, ng_ref, c_ref, m_ref):
    del prev8, next8
    nb = x.shape[0]
    inst = _instances(nb)

    @pl.when(first)
    def _():
        c_ref[...] = jnp.zeros(c_ref.shape, f32)
        m_ref[...] = jnp.zeros(m_ref.shape, f32)

    xf = x.reshape(nb * C, x.shape[2])
    q = xf[:, 0:G]
    k = xf[:, G:2 * G] * HD ** -0.5
    v = xf[:, 2 * G:3 * G]
    og = xf[:, 3 * G:4 * G]
    sc = xf[:, 4 * G:4 * G + SCALAR_W]
    li_all = sc + ib_ref[...]
    lf_all = _log_sigmoid(sc + fb_ref[...])
    incl = _before(d, False)
    b_all = _dot_exact_lhs(_before(d, False, nb * C, C).astype(f32), lf_all)
    b_rows = b_all.T
    li_rows = li_all.T
    bl_all = _last_rows(b_all, d, nb, C)
    lane = lax.broadcasted_iota(jnp.int32, (C, HD), 1)
    ones_col = (lane == 0).astype(f32)
    li_, lf_ = 4 * d, 8 + 4 * d
    qh = [q[_rs(b), _hs(h)] for b, h in inst]
    kh = [k[_rs(b), _hs(h)] for b, h in inst]
    vp = [jnp.concatenate([v[_rs(b), _hs(h)], ones_col], axis=1) for b, h in inst]
    bcol = [b_all[_rs(b), lf_ + h:lf_ + h + 1] for b, h in inst]
    brow = [b_rows[lf_ + h:lf_ + h + 1, _rs(b)] for b, h in inst]
    licol = [li_all[_rs(b), li_ + h:li_ + h + 1] for b, h in inst]
    lirow = [li_rows[li_ + h:li_ + h + 1, _rs(b)] for b, h in inst]
    bl = [bl_all[b][:, lf_ + h:lf_ + h + 1] for b, h in inst]
    m = [m_ref[b:b + 1, h:h + 1] for b, h in inst]
    dlog = [jnp.where(incl, bc_ - br_ + lr_, -jnp.inf) for bc_, br_, lr_ in zip(bcol, brow, lirow)]
    inter = [bc_ + m_ for bc_, m_ in zip(bcol, m)]
    m_i = [jnp.maximum(it_, jnp.max(dl_, -1, keepdims=True)) for it_, dl_ in zip(inter, dlog)]
    dw = [jnp.exp(dl_ - mi_) for dl_, mi_ in zip(dlog, m_i)]
    iw = [jnp.exp(it_ - mi_) for it_, mi_ in zip(inter, m_i)]
    s_ = [_dot(q_, k_, NT) * dw_ for q_, k_, dw_ in zip(qh, kh, dw)]
    cst = [c_ref[b * H + h] for b, h in inst]
    num = [iw_ * _dot(q_, c_) + _dot(s__, vp_) for iw_, q_, c_, s__, vp_ in zip(iw, qh, cst, s_, vp)]
    o_inst = [n_[:, :HD] / jnp.maximum(jnp.abs(n_[:, HD:HD + 1]), jnp.exp(-mi_)) for n_, mi_ in zip(num, m_i)]
    src = [bl_ - bc_ + lc_ for bl_, bc_, lc_ in zip(bl, bcol, licol)]
    m_new = [jnp.maximum(bl_ + m_, jnp.max(sr_, 0, keepdims=True)) for bl_, m_, sr_ in zip(bl, m, src)]
    for (b, h), bl_, m_, mn_, c_, k_, sr_, vp_ in zip(inst, bl, m, m_new, cst, kh, src, vp):
        c_ref[b * H + h] = jnp.exp(bl_ + m_ - mn_) * c_ + _dot(k_ * jnp.exp(sr_ - mn_), vp_, TN)
        m_ref[b:b + 1, h:h + 1] = mn_
    o = _assemble(o_inst, nb)
    if d == 0:
        return o
    return _head_rms(o + of, ng_ref[...]) * _sigmoid(og)


def _rwkv_body(d, x, prev8, next8, of, first, mu_ref, w0_ref, w2_ref, a0_ref, a2_ref, g2_ref,
               kk_ref, ka_ref, rk_ref, lng_ref, lnb_ref, s_ref, ext_ref):
    nb = x.shape[0]
    inst = _instances(nb)

    @pl.when(first)
    def _():
        s_ref[...] = jnp.zeros(s_ref.shape, f32)

    nbs = []
    for b in range(nb):
        ext_ref[b, 0:HALO, :] = prev8[b]
        ext_ref[b, HALO:HALO + C, :] = x[b]
        ext_ref[b, HALO + C:2 * HALO + C, :] = next8[b]
        nbs.append(ext_ref[b, pl.ds(HALO - 1, C), :] + ext_ref[b, pl.ds(HALO + 1, C), :])
    xf = x.reshape(nb * C, x.shape[2])
    xs = xf + mu_ref[...] * (0.5 * jnp.concatenate(nbs, axis=0) - xf)
    r = xs[:, 0:G]
    k = xs[:, G:2 * G]
    v = xs[:, 2 * G:3 * G]
    o_w = 3 * G
    o_a = o_w + 2 * RWKV_DECAY_RANK
    o_g = o_a + 2 * RWKV_AAA_RANK

    def a_of(dd):
        ad = xs[:, o_a + RWKV_AAA_RANK * dd:o_a + RWKV_AAA_RANK * (dd + 1)]
        return _sigmoid(a0_ref[dd:dd + 1, :] + _dot3(ad, a2_ref[dd]))

    wd = xs[:, o_w + RWKV_DECAY_RANK * d:o_w + RWKV_DECAY_RANK * (d + 1)]
    w = -_softplus(-(w0_ref[d:d + 1, :] + _dot3(jnp.tanh(wd), w2_ref[d]))) - 0.5
    logw = -jnp.exp(w)
    a = a_of(d)
    kk = _head_l2(k * kk_ref[...])
    kd = k * (1.0 + (a - 1.0) * ka_ref[...])
    beta = kk * a
    incl = _before(d, False)
    strict = _before(d, True)
    lc = _dot_exact_lhs(_before(d, False, nb * C, C).astype(f32), logw)
    tot_all = _last_rows(lc, d, nb, C)
    tot = jnp.concatenate([jnp.broadcast_to(t, (C, G)) for t in tot_all], axis=0)
    e_neg = jnp.exp(-lc)
    e_tot = jnp.exp(tot - lc)
    kp = kk * jnp.exp(lc - logw)
    rp = r * jnp.exp(lc)
    bn = beta * e_neg
    kn = kd * e_neg
    ke = kd * e_tot
    be = beta * e_tot
    sl = lambda t: [t[_rs(b), _hs(h)] for b, h in inst]
    kp_, rp_, bn_, kn_, ke_, be_, vh = sl(kp), sl(rp), sl(bn), sl(kn), sl(ke), sl(be), sl(v)
    kps = [_split(t) for t in kp_]
    ss = [s_ref[b * H + h] for b, h in inst]
    a_ab = [jnp.where(strict, _dot3s(p, _split(t), NT), 0.0) for p, t in zip(kps, bn_)]
    a_ak = [jnp.where(strict, _dot3s(p, _split(t), NT), 0.0) for p, t in zip(kps, kn_)]
    a_rb = [jnp.where(incl, _dot(p, t, NT), 0.0) for p, t in zip(rp_, bn_)]
    a_rk = [jnp.where(incl, _dot(p, t, NT), 0.0) for p, t in zip(rp_, kn_)]
    Ts = _tri_inverse_many(a_ab)
    rhs = [_dot3s(p, _split(s), NT) + _dot3(ak, v_) for p, s, ak, v_ in zip(kps, ss, a_ak, vh)]
    u = [_dot3(T, r_) for T, r_ in zip(Ts, rhs)]
    o_inst = [_dot(p, s, NT) - _dot(rb, u_) + _dot(rk, v_)
              for p, s, rb, u_, rk, v_ in zip(rp_, ss, a_rb, u, a_rk, vh)]
    for (b, h), s, v_, ke__, u_, be__ in zip(inst, ss, vh, ke_, u, be_):
        s_ref[b * H + h] = s * jnp.exp(tot_all[b][:, _hs(h)]) + _dot3(v_, ke__, TN) - _dot3(u_, be__, TN)
    o = _assemble(o_inst, nb)
    if d == 0:
        return o
    kd_other = k * (1.0 + (a_of(1 - d) - 1.0) * ka_ref[...])
    s_both = jnp.sum(_heads3(r * (kd + kd_other) * rk_ref[...]), -1, keepdims=True)
    bonus = (s_both * _heads3(v)).reshape(nb * C, G)
    t3 = _heads3(o + of)
    mu = jnp.mean(t3, -1, keepdims=True)
    tc = t3 - mu
    var = jnp.mean(tc * tc, -1, keepdims=True)
    gn = (tc * lax.rsqrt(var + RWKV_GN_EPS)).reshape(nb * C, G) * lng_ref[...] + lnb_ref[...]
    gate = _dot(_sigmoid(xs[:, o_g:o_g + RWKV_GATE_RANK]), g2_ref[...])
    return (gn + bonus) * gate


def _mixer_call(body, d, colmajor, use_halo, p_ctx, p_lat, of_ctx, of_lat, params, scratch, name):
    B, n_ctx, W = p_ctx.shape
    n_lat = p_lat.shape[1]
    nc, nl = n_ctx // C, n_lat // C
    if colmajor:
        assert n_lat == GRID_W * C
    have_of = d == 1

    def cj(s):
        return jnp.clip(s if d == 0 else nc - 1 - s, 0, nc - 1)

    def lj(s):
        return jnp.clip((s - nc) if d == 0 else nl - 1 - (s - nc), 0, nl - 1)

    def lat_view(t, w):
        return t.reshape(B, C, GRID_W * w) if colmajor else t

    def lat_spec(w):
        if colmajor:
            return pl.BlockSpec((B, C, w), lambda s: (0, 0, lj(s)))
        return pl.BlockSpec((B, C, w), lambda s: (0, lj(s), 0))

    def ctx_spec(w):
        return pl.BlockSpec((B, C, w), lambda s: (0, cj(s), 0))

    rpc = C // HALO
    operands = [p_ctx, lat_view(p_lat, W)]
    in_specs = [ctx_spec(W), lat_spec(W)]
    if use_halo:
        operands += [p_ctx, p_ctx, lat_view(p_lat, W), lat_view(p_lat, W)]
        in_specs += [
            pl.BlockSpec((B, HALO, W), lambda s: (0, jnp.maximum(rpc * cj(s) - 1, 0), 0)),
            pl.BlockSpec((B, HALO, W), lambda s: (0, jnp.minimum(rpc * (cj(s) + 1), rpc * nc - 1), 0)),
        ]
        if colmajor:
            in_specs += [
                pl.BlockSpec((B, HALO, W), lambda s: (0, rpc - 1, jnp.maximum(lj(s) - 1, 0))),
                pl.BlockSpec((B, HALO, W), lambda s: (0, 0, jnp.minimum(lj(s) + 1, nl - 1))),
            ]
        else:
            in_specs += [
                pl.BlockSpec((B, HALO, W), lambda s: (0, jnp.maximum(rpc * lj(s) - 1, 0), 0)),
                pl.BlockSpec((B, HALO, W), lambda s: (0, jnp.minimum(rpc * (lj(s) + 1), rpc * nl - 1), 0)),
            ]
    if have_of:
        operands += [of_ctx, lat_view(of_lat, G)]
        in_specs += [ctx_spec(G), lat_spec(G)]
    n_data = len(operands)
    for p in params:
        operands.append(p)
        in_specs.append(pl.BlockSpec(p.shape, lambda s, _n=p.ndim: (0,) * _n))
    n_par = len(params)

    def kern(*refs):
        data, par = refs[:n_data], refs[n_data:n_data + n_par]
        oc_ref, ol_ref = refs[n_data + n_par:n_data + n_par + 2]
        scr = refs[n_data + n_par + 2:]
        s = pl.program_id(0)
        is_ctx = s < nc
        x = jnp.where(is_ctx, data[0][...], data[1][...])
        i = 2
        prev8 = next8 = None
        if use_halo:
            j = jnp.where(is_ctx, cj(s), lj(s))
            jmax = jnp.where(is_ctx, nc - 1, nl - 1)
            prev8 = jnp.where(j > 0, jnp.where(is_ctx, data[2][...], data[4][...]), 0.0)
            next8 = jnp.where(j < jmax, jnp.where(is_ctx, data[3][...], data[5][...]), 0.0)
            i = 6
        of = jnp.where(is_ctx, data[i][...], data[i + 1][...]).reshape(B * C, G) if have_of else None
        y = body(d, x, prev8, next8, of, s == 0, *par, *scr).reshape(B, C, G)

        @pl.when(is_ctx)
        def _():
            oc_ref[...] = y

        @pl.when(jnp.logical_not(is_ctx))
        def _():
            ol_ref[...] = y

    out_ctx, out_lat = pl.pallas_call(
        kern,
        grid=(nc + nl,),
        in_specs=in_specs,
        out_specs=[ctx_spec(G), lat_spec(G)],
        out_shape=[jax.ShapeDtypeStruct((B, n_ctx, G), f32),
                   jax.ShapeDtypeStruct((B, C, GRID_W * G) if colmajor else (B, n_lat, G), f32)],
        scratch_shapes=scratch,
        compiler_params=pltpu.CompilerParams(dimension_semantics=("arbitrary",),
                                             vmem_limit_bytes=VMEM_LIMIT),
        name=name,
    )(*operands)
    return out_ctx, out_lat.reshape(B, n_lat, G)


def _run_mixer(body, colmajor, use_halo, p_ctx, p_lat, params, scratch, name):
    f_ctx, f_lat = _mixer_call(body, 0, colmajor, use_halo, p_ctx, p_lat, None, None, params, scratch, name + "_fwd")
    return _mixer_call(body, 1, colmajor, use_halo, p_ctx, p_lat, f_ctx, f_lat, params, scratch, name + "_bwd")


def _ada_call(cc, ada_w, ada_b):
    L, _, n6 = ada_w.shape
    tn = 1536

    def kern(c_ref, w_ref, b_ref, o_ref):
        cv = c_ref[...]
        o_ref[0] = _dot3(cv * _sigmoid(cv), w_ref[0]) + b_ref[0]

    return pl.pallas_call(
        kern,
        grid=(L, n6 // tn),
        in_specs=[pl.BlockSpec(cc.shape, lambda l, j: (0, 0)),
                  pl.BlockSpec((1, D_MODEL, tn), lambda l, j: (l, 0, j)),
                  pl.BlockSpec((1, 1, tn), lambda l, j: (l, 0, j))],
        out_specs=pl.BlockSpec((1, cc.shape[0], tn), lambda l, j: (l, 0, j)),
        out_shape=jax.ShapeDtypeStruct((L, cc.shape[0], n6), f32),
        compiler_params=pltpu.CompilerParams(dimension_semantics=("arbitrary", "arbitrary"),
                                             vmem_limit_bytes=VMEM_LIMIT),
        name="ada",
    )(cc, ada_w, ada_b.reshape(L, 1, n6))


def _mod_spec(per_batch, k):
    if per_batch:
        return pl.BlockSpec((1, 1, D_MODEL), lambda b, i: (b, 0, k))
    return pl.BlockSpec((1, 1, D_MODEL), lambda b, i: (0, 0, k))


def _const_spec(a):
    return pl.BlockSpec(a.shape, lambda b, i, _n=a.ndim: (0,) * _n)


def _proj_call(xin, mod, weights, per_batch, tm, name):
    B, n, _ = xin.shape
    widths = [w.shape[1] for w in weights]

    def kern(x_ref, sh_ref, sc_ref, *rest):
        w_refs, o_refs = rest[:len(weights)], rest[len(weights):]
        hm = (x_ref[0] * (1.0 + sc_ref[0]) + sh_ref[0]).astype(bf16)
        for w_ref, o_ref in zip(w_refs, o_refs):
            o_ref[0] = jnp.dot(hm, w_ref[...], preferred_element_type=f32)

    return pl.pallas_call(
        kern,
        grid=(B, n // tm),
        in_specs=[pl.BlockSpec((1, tm, D_MODEL), lambda b, i: (b, i, 0)),
                  _mod_spec(per_batch, 0), _mod_spec(per_batch, 1)] + [_const_spec(w) for w in weights],
        out_specs=[pl.BlockSpec((1, tm, w), lambda b, i: (b, i, 0)) for w in widths],
        out_shape=[jax.ShapeDtypeStruct((B, n, w), f32) for w in widths],
        compiler_params=pltpu.CompilerParams(dimension_semantics=("arbitrary", "arbitrary"),
                                             vmem_limit_bytes=VMEM_LIMIT),
        name=name,
    )(xin, mod, mod, *weights)


def _outproj_call(xin, ys, mod, w_out, ln_g, ln_b, per_batch, tm, name):
    B, n, _ = xin.shape

    def kern(x_ref, ya, yb, yc, yd, g_ref, w_ref, lg_ref, lb_ref, o_ref):
        acc = jnp.zeros((tm, D_MODEL), f32)
        for gi, y_ref in enumerate((ya, yb, yc, yd)):
            acc = acc + jnp.dot(y_ref[0].astype(bf16), w_ref[gi * G:(gi + 1) * G, :], preferred_element_type=f32)
        o_ref[0] = _layer_norm_rows(DEEPNORM_ALPHA * x_ref[0] + g_ref[0] * acc, lg_ref[...], lb_ref[...])

    yspec = pl.BlockSpec((1, tm, G), lambda b, i: (b, i, 0))
    return pl.pallas_call(
        kern,
        grid=(B, n // tm),
        in_specs=[pl.BlockSpec((1, tm, D_MODEL), lambda b, i: (b, i, 0)), yspec, yspec, yspec, yspec,
                  _mod_spec(per_batch, 2), _const_spec(w_out), _const_spec(ln_g), _const_spec(ln_b)],
        out_specs=pl.BlockSpec((1, tm, D_MODEL), lambda b, i: (b, i, 0)),
        out_shape=jax.ShapeDtypeStruct((B, n, D_MODEL), f32),
        compiler_params=pltpu.CompilerParams(dimension_semantics=("arbitrary", "arbitrary"),
                                             vmem_limit_bytes=VMEM_LIMIT),
        name=name,
    )(xin, *ys, mod, w_out, ln_g, ln_b)


def _mlp_call(xin, mod, w1, w2, ln_g, ln_b, per_batch, tm, name):
    B, n, _ = xin.shape
    tf = 1024

    def kern(x_ref, sh_ref, sc_ref, g_ref, w1_ref, w2_ref, lg_ref, lb_ref, o_ref):
        xv = x_ref[0]
        hm = (xv * (1.0 + sc_ref[0]) + sh_ref[0]).astype(bf16)
        acc = jnp.zeros((tm, D_MODEL), f32)
        for j in range(D_FF // tf):
            a = jnp.maximum(jnp.dot(hm, w1_ref[:, j * tf:(j + 1) * tf], preferred_element_type=f32), 0.0)
            acc = acc + jnp.dot((a * a).astype(bf16), w2_ref[j * tf:(j + 1) * tf, :], preferred_element_type=f32)
        o_ref[0] = _layer_norm_rows(DEEPNORM_ALPHA * xv + g_ref[0] * acc, lg_ref[...], lb_ref[...])

    return pl.pallas_call(
        kern,
        grid=(B, n // tm),
        in_specs=[pl.BlockSpec((1, tm, D_MODEL), lambda b, i: (b, i, 0)),
                  _mod_spec(per_batch, 3), _mod_spec(per_batch, 4), _mod_spec(per_batch, 5),
                  _const_spec(w1), _const_spec(w2), _const_spec(ln_g), _const_spec(ln_b)],
        out_specs=pl.BlockSpec((1, tm, D_MODEL), lambda b, i: (b, i, 0)),
        out_shape=jax.ShapeDtypeStruct((B, n, D_MODEL), f32),
        compiler_params=pltpu.CompilerParams(dimension_semantics=("arbitrary", "arbitrary"),
                                             vmem_limit_bytes=VMEM_LIMIT),
        name=name,
    )(xin, mod, mod, mod, w1, w2, ln_g, ln_b)


def _pad_lanes(t, start, width=SCALAR_W):
    flat = t.reshape(1, -1).astype(f32)
    return jnp.pad(flat, ((0, 0), (start, width - start - flat.shape[1])))


def _split_w_in(w):
    oa, ob, oc = W_A, W_A + W_B, W_A + W_B + (3 * G + 4 * H + G)
    wa, wb, wc, wd = w[:, :oa], w[:, oa:ob], w[:, ob:oc], w[:, oc:]

    def relayout(t):
        qkv, scal, gate = t[:, :3 * G], t[:, 3 * G:3 * G + 4 * H], t[:, 3 * G + 4 * H:]
        return jnp.concatenate([qkv, gate, scal, jnp.zeros((t.shape[0], SCALAR_W - 4 * H), t.dtype)], axis=1)

    return [t.astype(bf16) for t in (wa, wb, relayout(wc), relayout(wd))]


def kernel(x, c, ctx, c_ctx, ada_w, ada_b, w_in, w_out, ln1_g, ln1_b, ln2_g, ln2_b, mlp_w1, mlp_w2,
           hgrn_gamma, hgrn_norm_g, rwkv_mu, rwkv_w0, rwkv_w2, rwkv_a0, rwkv_a2, rwkv_g2, rwkv_k_k,
           rwkv_k_a, rwkv_r_k, rwkv_ln_g, rwkv_ln_b, gdn_conv, gdn_a_log, gdn_dt_bias, gdn_norm_g,
           mlstm_i_bias, mlstm_f_bias, mlstm_norm_g):
    B = x.shape[0]
    depth = ada_w.shape[0]
    cc = jnp.concatenate([c, c_ctx[None, :], jnp.zeros((8 - B - 1, D_MODEL), f32)], axis=0)
    mods = _ada_call(cc, ada_w, ada_b)
    lb_cum = jnp.cumsum(jax.nn.softmax(hgrn_gamma.astype(f32), axis=0), axis=0)
    row = lambda t: t.reshape(1, -1).astype(f32)
    state = lambda w: pltpu.VMEM((B * H, HD, w), f32)
    ext = lambda w: pltpu.VMEM((B, C + 2 * HALO, w), f32)

    for layer in range(depth):
        last = layer == depth - 1
        odd = layer % 2 == 1
        mod_lat = mods[layer, :B].reshape(B, 1, 6 * D_MODEL)
        mod_ctx = mods[layer, B:B + 1].reshape(1, 1, 6 * D_MODEL)
        w_groups = _split_w_in(w_in[layer])
        p_lat = _proj_call(x, mod_lat, w_groups, True, 256, "proj_lat")
        p_ctx = _proj_call(ctx, mod_ctx, w_groups, False, 256, "proj_ctx")

        lb = lb_cum[layer] - lb_cum[0]
        ys = [
            _run_mixer(_hgrn_body, odd, False, p_ctx[0], p_lat[0],
                       [lb, row(hgrn_norm_g[layer])], [state(HD)], "hgrn"),
            _run_mixer(_rwkv_body, odd, True, p_ctx[1], p_lat[1],
                       [row(rwkv_mu[layer]), rwkv_w0[layer], rwkv_w2[layer], rwkv_a0[layer], rwkv_a2[layer],
                        rwkv_g2[layer], row(rwkv_k_k[layer]), row(rwkv_k_a[layer]), row(rwkv_r_k[layer]),
                        row(rwkv_ln_g[layer]), row(rwkv_ln_b[layer])],
                       [state(HD), ext(W_B)], "rwkv"),
            _run_mixer(_gdn_body, not odd, True, p_ctx[2], p_lat[2],
                       [gdn_conv[layer], _pad_lanes(gdn_a_log[layer], 2 * H), _pad_lanes(gdn_dt_bias[layer], 2 * H),
                        row(gdn_norm_g[layer])],
                       [state(HD), ext(3 * G)], "gdn"),
            _run_mixer(_mlstm_body, not odd, False, p_ctx[3], p_lat[3],
                       [_pad_lanes(mlstm_i_bias[layer], 0), _pad_lanes(mlstm_f_bias[layer], 2 * H),
                        row(mlstm_norm_g[layer])],
                       [state(2 * HD), pltpu.VMEM((8, SCALAR_W), f32)], "mlstm"),
        ]
        wo = w_out[layer].astype(bf16)
        w1 = mlp_w1[layer].astype(bf16)
        w2 = mlp_w2[layer].astype(bf16)
        g1, b1, g2, b2 = row(ln1_g[layer]), row(ln1_b[layer]), row(ln2_g[layer]), row(ln2_b[layer])
        x = _outproj_call(x, [y[1] for y in ys], mod_lat, wo, g1, b1, True, 256, "outproj_lat")
        x = _mlp_call(x, mod_lat, w1, w2, g2, b2, True, 256, "mlp_lat")
        if not last:
            ctx = _outproj_call(ctx, [y[0] for y in ys], mod_ctx, wo, g1, b1, False, 256, "outproj_ctx")
            ctx = _mlp_call(ctx, mod_ctx, w1, w2, g2, b2, False, 256, "mlp_ctx")
    return x
```

```python
import math

import jax
import jax.numpy as jnp
from jax import lax
from jax.experimental import pallas as pl
from jax.experimental.pallas import tpu as pltpu

f32 = jnp.float32
bf16 = jnp.bfloat16

D_MODEL = 1024
DEPTH = 4
GRID_W = 64
GROUP_W = 256
HEAD_DIM = 64
GROUP_HEADS = 4
D_FF = 4 * D_MODEL
CHUNK = 64
SUB = 16
HALO = 8
RWKV_DECAY_RANK = 64
RWKV_AAA_RANK = 64
RWKV_GATE_RANK = 128
GDN_CONV_W = 5
LN_EPS = 1e-5
RWKV_GN_EPS = 64e-5
NORM_EPS = 1e-6
DEEPNORM_ALPHA = (2.0 * DEPTH) ** 0.25
SCALAR_W = 128
W_A = 5 * GROUP_W
W_B = 3 * GROUP_W + 2 * RWKV_DECAY_RANK + 2 * RWKV_AAA_RANK + RWKV_GATE_RANK
W_CD = 4 * GROUP_W + SCALAR_W
VMEM_LIMIT = 56 * 1024 * 1024

NN = (((1,), (0,)), ((), ()))
NT = (((1,), (1,)), ((), ()))
TN = (((0,), (0,)), ((), ()))

G, H, HD, C = GROUP_W, GROUP_HEADS, HEAD_DIM, CHUNK
P2 = 2 * HD
NP = H // 2
assert HD == C and P2 == 128


def _mm(a, b, dims=NN):
    return lax.dot_general(a, b, dims, preferred_element_type=f32)


def _dot(a, b, dims=NN):
    return _mm(a.astype(bf16), b.astype(bf16), dims)


def _split(x):
    hi = x.astype(bf16)
    lo = (x - hi.astype(f32)).astype(bf16)
    return hi, lo


def _dot3s(a, b, dims=NN):
    return _mm(a[0], b[0], dims) + (_mm(a[0], b[1], dims) + _mm(a[1], b[0], dims))


def _dot3(a, b, dims=NN):
    return _dot3s(_split(a), _split(b), dims)


def _split_exact(x):
    h = x.astype(bf16)
    r = x - h.astype(f32)
    mid = r.astype(bf16)
    lo = (r - mid.astype(f32)).astype(bf16)
    return h, mid, lo


def _dot_exact_lhs(m, x):
    mb = m.astype(bf16)
    h, mid, lo = _split_exact(x)
    return _mm(mb, h) + (_mm(mb, mid) + _mm(mb, lo))


def _dot_exact_rhs(x, m):
    mb = m.astype(bf16)
    h, mid, lo = _split_exact(x)
    return _mm(h, mb) + (_mm(mid, mb) + _mm(lo, mb))


def _head_sum(t):
    r, c = _iota2(G)
    return _dot_exact_rhs(t, _same_block(r, c, HD))


def _expand_heads(t, lane0):
    r = lax.broadcasted_iota(jnp.int32, (SCALAR_W, G), 0)
    c = lax.broadcasted_iota(jnp.int32, (SCALAR_W, G), 1)
    return _dot_exact_rhs(t, r == lane0 + jnp.right_shift(c, int(math.log2(HD))))


def _softplus(x):
    return jnp.maximum(x, 0.0) + jnp.log1p(jnp.exp(-jnp.abs(x)))


def _log_sigmoid(x):
    return -_softplus(-x)


def _sigmoid(x):
    return jax.nn.sigmoid(x)


def _iota2(n):
    return (lax.broadcasted_iota(jnp.int32, (n, n), 0), lax.broadcasted_iota(jnp.int32, (n, n), 1))


def _same_block(r, c, size):
    sh = int(math.log2(size))
    return jnp.right_shift(r, sh) == jnp.right_shift(c, sh)


def _before(d, strict, n=C, block=None):
    r, c = _iota2(n)
    if d == 0:
        m = (c < r) if strict else (c <= r)
    else:
        m = (c > r) if strict else (c >= r)
    if block is not None and block != n:
        m = jnp.logical_and(m, _same_block(r, c, block))
    return m


def _heads3(t):
    return t.reshape(t.shape[0], H, HD)


def _head_l2(t):
    return t * lax.rsqrt(_head_sum(t * t) + 1e-12)


def _head_rms(t, g):
    return t * lax.rsqrt(_head_sum(t * t) * (1.0 / HD) + NORM_EPS) * g


def _pk_iota(n):
    r = lax.broadcasted_iota(jnp.int32, (n, P2), 0)
    c = jnp.bitwise_and(lax.broadcasted_iota(jnp.int32, (n, P2), 1), HD - 1)
    return r, c


def _pk_before(d, strict):
    r, c = _pk_iota(C)
    if d == 0:
        return (c < r) if strict else (c <= r)
    return (c > r) if strict else (c >= r)


def _bd(t):
    left = lax.broadcasted_iota(jnp.int32, t.shape, 1) < HD
    return jnp.concatenate([jnp.where(left, t, 0.0), jnp.where(left, 0.0, t)], axis=0).astype(bf16)


def _bd_mask(t):
    r = jnp.right_shift(lax.broadcasted_iota(jnp.int32, t.shape, 0), int(math.log2(HD)))
    c = jnp.bitwise_and(jnp.right_shift(lax.broadcasted_iota(jnp.int32, t.shape, 1), int(math.log2(HD))), 1)
    return jnp.where(r == c, t, 0.0)


def _tri_inverse_many(Ls):
    r, c = _pk_iota(C)
    eye = (r == c).astype(f32)
    blk8 = _same_block(r, c, 8)
    ps = [-jnp.where(blk8, L, 0.0) for L in Ls]
    xs = [eye + p for p in ps]
    for _ in range(2):
        ps = [_mm(p.astype(bf16), _bd(p)) for p in ps]
        xs = [x + _mm(x.astype(bf16), _bd(p)) for x, p in zip(xs, ps)]
    for s in (8, 16, 32):
        ring = jnp.logical_and(_same_block(r, c, 2 * s), jnp.logical_not(_same_block(r, c, s)))
        ts = [_mm(x.astype(bf16), _bd(jnp.where(ring, L, 0.0))) for x, L in zip(xs, Ls)]
        xs = [x - _mm(t.astype(bf16), _bd(x)) for x, t in zip(xs, ts)]
    return xs


def _layer_norm_rows(t, g, b):
    mu = jnp.mean(t, -1, keepdims=True)
    tc = t - mu
    var = jnp.mean(tc * tc, -1, keepdims=True)
    return tc * lax.rsqrt(var + LN_EPS) * g + b


def _last_rows(t, d, nb, rows):
    return [t[rows * bb + (rows - 1 if d == 0 else 0):rows * bb + (rows if d == 0 else 1)] for bb in range(nb)]


def _instances(nb):
    return [(bb, p) for bb in range(nb) for p in range(NP)]


def _rs(bb):
    return slice(C * bb, C * bb + C)


def _ps(p):
    return slice(P2 * p, P2 * p + P2)


def _pk_rows(rows_t, lane0, bb, p):
    h = lane0 + 2 * p
    return jnp.concatenate([rows_t[h:h + 1, _rs(bb)], rows_t[h + 1:h + 2, _rs(bb)]], axis=1)


def _assemble(o_inst, nb):
    return jnp.concatenate([jnp.concatenate(o_inst[bb * NP:(bb + 1) * NP], axis=1) for bb in range(nb)], axis=0)


def _hgrn_body(d, x, prev8, next8, of, first, lb_ref, ng_ref, s_ref):
    del prev8, next8
    nb = x.shape[0]
    inst = _instances(nb)

    @pl.when(first)
    def _():
        s_ref[...] = jnp.zeros(s_ref.shape, f32)

    xf = x.reshape(nb * C, x.shape[2])
    q = xf[:, 0:G]
    v = xf[:, G:2 * G]
    f = xf[:, (2 + d) * G:(3 + d) * G]
    g = xf[:, 4 * G:5 * G]
    lb = lb_ref[d:d + 1, :]
    a = jnp.log(lb)
    bb_ = jnp.log1p(-lb) + _log_sigmoid(f)
    log_f = jnp.maximum(a, bb_) + jnp.log1p(jnp.exp(-jnp.abs(a - bb_)))
    k = (1.0 - lb) * _sigmoid(-f)

    ns = nb * SUB
    m16 = _before(d, False, ns, SUB).astype(f32)
    lr, lc_ = _iota2(G)
    seg = _same_block(lr, lc_, HD).astype(bf16)
    ri = lax.broadcasted_iota(jnp.int32, (nb, SUB, G), 1)
    n_sub = C // SUB
    o_sub = [None] * n_sub
    for sbi in range(n_sub):
        sb = sbi if d == 0 else n_sub - 1 - sbi
        pick = lambda t: jnp.concatenate([t[C * b + SUB * sb:C * b + SUB * sb + SUB] for b in range(nb)], axis=0)
        lf_s, q_s, k_s, v_s = pick(log_f), pick(q), pick(k), pick(v)
        cum = _dot_exact_lhs(m16, lf_s)
        cum3, q3, k3, v3 = (t.reshape(nb, SUB, G) for t in (cum, q_s, k_s, v_s))
        tot3 = cum3[:, SUB - 1:SUB, :] if d == 0 else cum3[:, 0:1, :]
        qe3 = q3 * jnp.exp(cum3)
        ke3 = k3 * jnp.exp(tot3 - cum3)
        ts = []
        for j in range(SUB):
            e = jnp.exp(jnp.minimum(cum3 - cum3[:, j:j + 1, :], 0.0))
            valid = (ri >= j) if d == 0 else (ri <= j)
            ts.append(jnp.where(valid, q3 * e * k3[:, j:j + 1, :], 0.0).astype(bf16).reshape(ns, G))
        rsum = _mm(jnp.concatenate(ts, axis=0), seg)
        od = jnp.zeros((nb, SUB, G), f32)
        for j in range(SUB):
            od = od + rsum[j * ns:(j + 1) * ns].reshape(nb, SUB, G) * v3[:, j:j + 1, :]
        ss = [s_ref[b * NP + p] for b, p in inst]
        oi = [_dot(qe3[b][:, _ps(p)], s, NT) for (b, p), s in zip(inst, ss)]
        for (b, p), s in zip(inst, ss):
            s_ref[b * NP + p] = (s * jnp.exp(tot3[b][:, _ps(p)])
                                 + _bd_mask(_dot(v3[b][:, _ps(p)], ke3[b][:, _ps(p)], TN)))
        o_sub[sb] = [jnp.concatenate(oi[b * NP:(b + 1) * NP], axis=1) + od[b] for b in range(nb)]
    o = jnp.concatenate([o_sub[sb][b] for b in range(nb) for sb in range(n_sub)], axis=0)
    if d == 0:
        return o
    return _head_rms(o + of, ng_ref[...]) * (g * _sigmoid(g))


def _gdn_body(d, x, prev8, next8, of, first, cw_ref, alog_ref, dtb_ref, ng_ref, s_ref, ext_ref):
    nb = x.shape[0]
    inst = _instances(nb)

    @pl.when(first)
    def _():
        s_ref[...] = jnp.zeros(s_ref.shape, f32)

    wq = 3 * G
    accs = []
    for b in range(nb):
        ext_ref[b, 0:HALO, :] = prev8[b][:, :wq]
        ext_ref[b, HALO:HALO + C, :] = x[b][:, :wq]
        ext_ref[b, HALO + C:2 * HALO + C, :] = next8[b][:, :wq]
        acc = jnp.zeros((C, wq), f32)
        for kk in range(GDN_CONV_W):
            acc = acc + cw_ref[kk:kk + 1, :] * ext_ref[b, pl.ds(HALO - GDN_CONV_W // 2 + kk, C), :]
        accs.append(acc)
    acc = jnp.concatenate(accs, axis=0)
    xf = x.reshape(nb * C, x.shape[2])
    qkv = acc * _sigmoid(acc)
    q = _head_l2(qkv[:, 0:G]) * HD ** -0.5
    k = _head_l2(qkv[:, G:2 * G])
    v = qkv[:, 2 * G:3 * G]
    g = xf[:, 3 * G:4 * G]
    sc = xf[:, 4 * G:4 * G + SCALAR_W]
    beta_all = _sigmoid(sc)
    la_all = -jnp.exp(alog_ref[...]) * _softplus(sc + dtb_ref[...])
    incl = _pk_before(d, False)
    strict = _pk_before(d, True)
    cum_all = _dot_exact_lhs(_before(d, False, nb * C, C).astype(f32), la_all)
    cum_rows = cum_all.T
    last_bc = jnp.concatenate([jnp.broadcast_to(t, (C, SCALAR_W)) for t in _last_rows(cum_all, d, nb, C)], axis=0)
    lb_, la_ = 4 * d, 8 + 4 * d
    nr = nb * C
    beta_e = _expand_heads(beta_all, lb_)
    cum_ld = _expand_heads(jnp.concatenate([cum_all, last_bc - cum_all], axis=0), la_)
    cum_e = cum_ld[:nr]
    ec_e = jnp.exp(cum_e)
    kb_f = k * beta_e
    vb_f = v * beta_e
    kbe_f = kb_f * ec_e
    qe_f = q * ec_e
    kl_f = k * jnp.exp(cum_ld[nr:])
    sl = lambda t: [t[_rs(b), _ps(p)] for b, p in inst]
    qh, kh, kb, vb, kbe, qe, kl, cc = sl(q), sl(k), sl(kb_f), sl(vb_f), sl(kbe_f), sl(qe_f), sl(kl_f), sl(cum_e)
    cr = [_pk_rows(cum_rows, la_, b, p) for b, p in inst]
    row_last = C - 1 if d == 0 else 0
    e_last = [ec_e[C * b + row_last:C * b + row_last + 1, _ps(p)] for b, p in inst]
    gam = [jnp.where(incl, jnp.exp(jnp.minimum(c_ - r_, 0.0)), 0.0) for c_, r_ in zip(cc, cr)]
    kk_qk = [_mm(jnp.concatenate([kb_, q_], axis=0).astype(bf16), _bd(k_), NT) * jnp.concatenate([g_, g_], axis=0)
             for kb_, q_, k_, g_ in zip(kb, qh, kh, gam)]
    Ts = _tri_inverse_many([jnp.where(strict, a_[:C], 0.0) for a_ in kk_qk])
    sol = [_mm(T.astype(bf16), jnp.concatenate([_bd(vb_), _bd(kbe_)], axis=1)) for T, vb_, kbe_ in zip(Ts, vb, kbe)]
    ss = [s_ref[b * NP + p] for b, p in inst]
    ws_qs = [_dot(jnp.concatenate([so[:, P2:], qe_], axis=0), s) for so, qe_, s in zip(sol, qe, ss)]
    vn = [so[:, :P2] - t_[:C] for so, t_ in zip(sol, ws_qs)]
    o_inst = [t_[C:] + _mm(a_[C:].astype(bf16), _bd(vn_)) for t_, a_, vn_ in zip(ws_qs, kk_qk, vn)]
    for (b, p), s, el_, kl_, vn_ in zip(inst, ss, e_last, kl, vn):
        s_ref[b * NP + p] = el_ * s + _bd_mask(_dot(kl_, vn_, TN))
    o = _assemble(o_inst, nb)
    if d == 0:
        return o
    return _head_rms(o + of, ng_ref[...]) * (g * _sigmoid(g))


def _mlstm_body(d, x, prev8, next8, of, first, ib_ref, fb_ref, ng_ref, c_ref, m_ref):
    del prev8, next8
    nb = x.shape[0]
    inst = _instances(nb)

    @pl.when(first)
    def _():
        c_ref[...] = jnp.zeros(c_ref.shape, f32)
        m_ref[...] = jnp.zeros(m_ref.shape, f32)

    xf = x.reshape(nb * C, x.shape[2])
    q = xf[:, 0:G]
    k = xf[:, G:2 * G] * HD ** -0.5
    v = xf[:, 2 * G:3 * G]
    og = xf[:, 3 * G:4 * G]
    sc = xf[:, 4 * G:4 * G + SCALAR_W]
    li_all = sc + ib_ref[...]
    lf_all = _log_sigmoid(sc + fb_ref[...])
    incl = _pk_before(d, False)
    nr = nb * C
    stack = lambda rows: jnp.concatenate([jnp.broadcast_to(t, (C, SCALAR_W)) for t in rows], axis=0)
    b_all = _dot_exact_lhs(_before(d, False, nr, C).astype(f32), lf_all)
    li_sh = pltpu.roll(li_all, 2 * H, axis=1)
    m_rows = [m_ref[b:b + 1, :] for b in range(nb)]
    m_all = stack(m_rows)
    a_all = li_sh - b_all
    loc = jnp.bitwise_and(lax.broadcasted_iota(jnp.int32, (nr, SCALAR_W), 0), C - 1)
    run = a_all
    sh = 1
    while sh < C:
        if d == 0:
            shifted = jnp.where(loc >= sh, pltpu.roll(run, sh, axis=0), -jnp.inf)
        else:
            shifted = jnp.where(loc < C - sh, pltpu.roll(run, nr - sh, axis=0), -jnp.inf)
        run = jnp.maximum(run, shifted)
        sh *= 2
    mx_all = jnp.maximum(run, m_all)
    iw_all = jnp.exp(m_all - mx_all)
    eneg_all = jnp.exp(-(b_all + mx_all))
    bl_rows = _last_rows(b_all, d, nb, C)
    src_all = stack(bl_rows) - b_all + li_sh
    mnew_rows = [jnp.maximum(bl_ + m_, jnp.max(src_all[_rs(b)], 0, keepdims=True))
                 for b, (bl_, m_) in enumerate(zip(bl_rows, m_rows))]
    cw_rows = [jnp.exp(bl_ + m_ - mn_) for bl_, m_, mn_ in zip(bl_rows, m_rows, mnew_rows)]
    wj_all = jnp.exp(src_all - stack(mnew_rows))
    a_rows = a_all.T
    ln = 2 * H + 4 * d
    ex = _expand_heads(jnp.concatenate([mx_all, iw_all, eneg_all, wj_all, stack(cw_rows)], axis=0), ln)
    mx_e, iw_e, en_e, wj_e, cw_e = (ex[i * nr:(i + 1) * nr] for i in range(5))
    sl = lambda t: [t[_rs(b), _ps(p)] for b, p in inst]
    ones = jnp.ones((C, P2), f32)
    bd_ones = _bd(ones)
    qh, qi, kw, mxc, enc = sl(q), sl(q * iw_e), sl(k * wj_e), sl(mx_e), sl(en_e)
    kh, vh = sl(k), sl(v)
    arow = [_pk_rows(a_rows, ln, b, p) for b, p in inst]
    cw = [cw_e[C * b:C * b + 1, _ps(p)] for b, p in inst]
    dw = [jnp.where(incl, jnp.exp(jnp.minimum(ar_ - mc_, 0.0)), 0.0) for ar_, mc_ in zip(arow, mxc)]
    s_ = [_mm(q_.astype(bf16), _bd(k_), NT) * dw_ for q_, k_, dw_ in zip(qh, kh, dw)]
    cst = [c_ref[b * NP + p] for b, p in inst]
    num = [_dot(qi_, c_) + _mm(s__.astype(bf16), jnp.concatenate([_bd(v_), bd_ones], axis=1))
           for qi_, c_, s__, v_ in zip(qi, cst, s_, vh)]
    o_inst = [n_[:, :P2] * (1.0 / jnp.maximum(jnp.abs(n_[:, P2:]), en_)) for n_, en_ in zip(num, enc)]
    for (b, p), c_, cw_, kw_, v_ in zip(inst, cst, cw, kw, vh):
        c_ref[b * NP + p] = (jnp.concatenate([cw_, cw_], axis=1) * c_
                             + _bd_mask(_dot(kw_, jnp.concatenate([v_, ones], axis=1), TN)))
    for b in range(nb):
        m_ref[b:b + 1, :] = mnew_rows[b]
    o = _assemble(o_inst, nb)
    if d == 0:
        return o
    return _head_rms(o + of, ng_ref[...]) * _sigmoid(og)


def _rwkv_body(d, x, prev8, next8, of, first, mu_ref, w0_ref, w2_ref, a0_ref, a2_ref, g2_ref,
               kk_ref, ka_ref, rk_ref, lng_ref, lnb_ref, s_ref, ext_ref):
    nb = x.shape[0]
    inst = _instances(nb)

    @pl.when(first)
    def _():
        s_ref[...] = jnp.zeros(s_ref.shape, f32)

    nbs = []
    for b in range(nb):
        ext_ref[b, 0:HALO, :] = prev8[b]
        ext_ref[b, HALO:HALO + C, :] = x[b]
        ext_ref[b, HALO + C:2 * HALO + C, :] = next8[b]
        nbs.append(ext_ref[b, pl.ds(HALO - 1, C), :] + ext_ref[b, pl.ds(HALO + 1, C), :])
    xf = x.reshape(nb * C, x.shape[2])
    xs = xf + mu_ref[...] * (0.5 * jnp.concatenate(nbs, axis=0) - xf)
    r = xs[:, 0:G]
    k = xs[:, G:2 * G]
    v = xs[:, 2 * G:3 * G]
    o_w = 3 * G
    o_a = o_w + 2 * RWKV_DECAY_RANK
    o_g = o_a + 2 * RWKV_AAA_RANK

    def a_of(dd):
        ad = xs[:, o_a + RWKV_AAA_RANK * dd:o_a + RWKV_AAA_RANK * (dd + 1)]
        return _sigmoid(a0_ref[dd:dd + 1, :] + _dot3(ad, a2_ref[dd]))

    wd = xs[:, o_w + RWKV_DECAY_RANK * d:o_w + RWKV_DECAY_RANK * (d + 1)]
    w = -_softplus(-(w0_ref[d:d + 1, :] + _dot3(jnp.tanh(wd), w2_ref[d]))) - 0.5
    logw = -jnp.exp(w)
    a = a_of(d)
    kk = _head_l2(k * kk_ref[...])
    kd = k * (1.0 + (a - 1.0) * ka_ref[...])
    beta = kk * a
    incl = _pk_before(d, False)
    strict = _pk_before(d, True)
    lc = _dot_exact_lhs(_before(d, False, nb * C, C).astype(f32), logw)
    tot_all = _last_rows(lc, d, nb, C)
    tot = jnp.concatenate([jnp.broadcast_to(t, (C, G)) for t in tot_all], axis=0)
    e_neg = jnp.exp(-lc)
    e_tot = jnp.exp(tot - lc)
    kp = kk * jnp.exp(lc - logw)
    rp = r * jnp.exp(lc)
    bn = beta * e_neg
    kn = kd * e_neg
    ke = kd * e_tot
    be = beta * e_tot
    sl = lambda t: [t[_rs(b), _ps(p)] for b, p in inst]
    kp_, rp_, bn_, kn_, ke_, be_, vh = sl(kp), sl(rp), sl(bn), sl(kn), sl(ke), sl(be), sl(v)
    ss = [s_ref[b * NP + p] for b, p in inst]
    lhs = [jnp.concatenate([p_, t], axis=0).astype(bf16) for p_, t in zip(kp_, rp_)]
    quad = [_mm(l_, jnp.concatenate([_bd(b_), _bd(k_)], axis=0), NT) for l_, b_, k_ in zip(lhs, bn_, kn_)]
    Ts = _tri_inverse_many([jnp.where(strict, q_[:C, :P2], 0.0) for q_ in quad])
    ps_rs = [_mm(l_, s.astype(bf16), NT) for l_, s in zip(lhs, ss)]
    rhs = [t_[:C] + _mm(jnp.where(strict, q_[:C, P2:], 0.0).astype(bf16), _bd(v_))
           for t_, q_, v_ in zip(ps_rs, quad, vh)]
    u = [_mm(T.astype(bf16), _bd(r_)) for T, r_ in zip(Ts, rhs)]
    o_inst = [t_[C:] + _mm(jnp.concatenate([jnp.where(incl, q_[C:, P2:], 0.0), jnp.where(incl, -q_[C:, :P2], 0.0)],
                                           axis=1).astype(bf16),
                           jnp.concatenate([_bd(v_), _bd(u_)], axis=0))
              for t_, q_, v_, u_ in zip(ps_rs, quad, vh, u)]
    for (b, p), s, v_, ke__, u_, be__ in zip(inst, ss, vh, ke_, u, be_):
        s_ref[b * NP + p] = (s * jnp.exp(tot_all[b][:, _ps(p)])
                             + _bd_mask(_dot(jnp.concatenate([v_, -u_], axis=0),
                                             jnp.concatenate([ke__, be__], axis=0), TN)))
    o = _assemble(o_inst, nb)
    if d == 0:
        return o
    kd_other = k * (1.0 + (a_of(1 - d) - 1.0) * ka_ref[...])
    bonus = _head_sum(r * (kd + kd_other) * rk_ref[...]) * v
    t = o + of
    tc = t - _head_sum(t) * (1.0 / HD)
    var = _head_sum(tc * tc) * (1.0 / HD)
    gn = tc * lax.rsqrt(var + RWKV_GN_EPS) * lng_ref[...] + lnb_ref[...]
    gate = _dot(_sigmoid(xs[:, o_g:o_g + RWKV_GATE_RANK]), g2_ref[...])
    return (gn + bonus) * gate


def _mixer_call(body, d, colmajor, use_halo, p_ctx, p_lat, of_ctx, of_lat, params, scratch, name):
    B, n_ctx, W = p_ctx.shape
    n_lat = p_lat.shape[1]
    nc, nl = n_ctx // C, n_lat // C
    if colmajor:
        assert n_lat == GRID_W * C
    have_of = d == 1

    def cj(s):
        return jnp.clip(s if d == 0 else nc - 1 - s, 0, nc - 1)

    def lj(s):
        return jnp.clip((s - nc) if d == 0 else nl - 1 - (s - nc), 0, nl - 1)

    def lat_view(t, w):
        return t.reshape(B, C, GRID_W * w) if colmajor else t

    def lat_spec(w):
        if colmajor:
            return pl.BlockSpec((B, C, w), lambda s: (0, 0, lj(s)))
        return pl.BlockSpec((B, C, w), lambda s: (0, lj(s), 0))

    def ctx_spec(w):
        return pl.BlockSpec((B, C, w), lambda s: (0, cj(s), 0))

    rpc = C // HALO
    operands = [p_ctx, lat_view(p_lat, W)]
    in_specs = [ctx_spec(W), lat_spec(W)]
    if use_halo:
        operands += [p_ctx, p_ctx, lat_view(p_lat, W), lat_view(p_lat, W)]
        in_specs += [
            pl.BlockSpec((B, HALO, W), lambda s: (0, jnp.maximum(rpc * cj(s) - 1, 0), 0)),
            pl.BlockSpec((B, HALO, W), lambda s: (0, jnp.minimum(rpc * (cj(s) + 1), rpc * nc - 1), 0)),
        ]
        if colmajor:
            in_specs += [
                pl.BlockSpec((B, HALO, W), lambda s: (0, rpc - 1, jnp.maximum(lj(s) - 1, 0))),
                pl.BlockSpec((B, HALO, W), lambda s: (0, 0, jnp.minimum(lj(s) + 1, nl - 1))),
            ]
        else:
            in_specs += [
                pl.BlockSpec((B, HALO, W), lambda s: (0, jnp.maximum(rpc * lj(s) - 1, 0), 0)),
                pl.BlockSpec((B, HALO, W), lambda s: (0, jnp.minimum(rpc * (lj(s) + 1), rpc * nl - 1), 0)),
            ]
    if have_of:
        operands += [of_ctx, lat_view(of_lat, G)]
        in_specs += [ctx_spec(G), lat_spec(G)]
    n_data = len(operands)
    for p in params:
        operands.append(p)
        in_specs.append(pl.BlockSpec(p.shape, lambda s, _n=p.ndim: (0,) * _n))
    n_par = len(params)

    def kern(*refs):
        data, par = refs[:n_data], refs[n_data:n_data + n_par]
        oc_ref, ol_ref = refs[n_data + n_par:n_data + n_par + 2]
        scr = refs[n_data + n_par + 2:]
        s = pl.program_id(0)
        is_ctx = s < nc
        x = jnp.where(is_ctx, data[0][...], data[1][...])
        i = 2
        prev8 = next8 = None
        if use_halo:
            j = jnp.where(is_ctx, cj(s), lj(s))
            jmax = jnp.where(is_ctx, nc - 1, nl - 1)
            prev8 = jnp.where(j > 0, jnp.where(is_ctx, data[2][...], data[4][...]), 0.0)
            next8 = jnp.where(j < jmax, jnp.where(is_ctx, data[3][...], data[5][...]), 0.0)
            i = 6
        of = jnp.where(is_ctx, data[i][...], data[i + 1][...]).reshape(B * C, G) if have_of else None
        y = body(d, x, prev8, next8, of, s == 0, *par, *scr).reshape(B, C, G)

        @pl.when(is_ctx)
        def _():
            oc_ref[...] = y

        @pl.when(jnp.logical_not(is_ctx))
        def _():
            ol_ref[...] = y

    out_ctx, out_lat = pl.pallas_call(
        kern,
        grid=(nc + nl,),
        in_specs=in_specs,
        out_specs=[ctx_spec(G), lat_spec(G)],
        out_shape=[jax.ShapeDtypeStruct((B, n_ctx, G), f32),
                   jax.ShapeDtypeStruct((B, C, GRID_W * G) if colmajor else (B, n_lat, G), f32)],
        scratch_shapes=scratch,
        compiler_params=pltpu.CompilerParams(dimension_semantics=("arbitrary",),
                                             vmem_limit_bytes=VMEM_LIMIT),
        name=name,
    )(*operands)
    return out_ctx, out_lat.reshape(B, n_lat, G)


def _run_mixer(body, colmajor, use_halo, p_ctx, p_lat, params, scratch, name):
    f_ctx, f_lat = _mixer_call(body, 0, colmajor, use_halo, p_ctx, p_lat, None, None, params, scratch, name + "_fwd")
    return _mixer_call(body, 1, colmajor, use_halo, p_ctx, p_lat, f_ctx, f_lat, params, scratch, name + "_bwd")


def _ada_call(cc, ada_w, ada_b):
    L, _, n6 = ada_w.shape
    tn = 1536

    def kern(c_ref, w_ref, b_ref, o_ref):
        cv = c_ref[...]
        o_ref[0] = _dot3(cv * _sigmoid(cv), w_ref[0]) + b_ref[0]

    return pl.pallas_call(
        kern,
        grid=(L, n6 // tn),
        in_specs=[pl.BlockSpec(cc.shape, lambda l, j: (0, 0)),
                  pl.BlockSpec((1, D_MODEL, tn), lambda l, j: (l, 0, j)),
                  pl.BlockSpec((1, 1, tn), lambda l, j: (l, 0, j))],
        out_specs=pl.BlockSpec((1, cc.shape[0], tn), lambda l, j: (l, 0, j)),
        out_shape=jax.ShapeDtypeStruct((L, cc.shape[0], n6), f32),
        compiler_params=pltpu.CompilerParams(dimension_semantics=("arbitrary", "arbitrary"),
                                             vmem_limit_bytes=VMEM_LIMIT),
        name="ada",
    )(cc, ada_w, ada_b.reshape(L, 1, n6))


def _mod_spec(per_batch, k):
    if per_batch:
        return pl.BlockSpec((1, 1, D_MODEL), lambda b, i: (b, 0, k))
    return pl.BlockSpec((1, 1, D_MODEL), lambda b, i: (0, 0, k))


def _const_spec(a):
    return pl.BlockSpec(a.shape, lambda b, i, _n=a.ndim: (0,) * _n)


def _proj_call(xin, mod, weights, per_batch, tm, name):
    B, n, _ = xin.shape
    widths = [w.shape[1] for w in weights]

    def kern(x_ref, sh_ref, sc_ref, *rest):
        w_refs, o_refs = rest[:len(weights)], rest[len(weights):]
        hm = (x_ref[0] * (1.0 + sc_ref[0]) + sh_ref[0]).astype(bf16)
        for w_ref, o_ref in zip(w_refs, o_refs):
            o_ref[0] = jnp.dot(hm, w_ref[...], preferred_element_type=f32)

    return pl.pallas_call(
        kern,
        grid=(B, n // tm),
        in_specs=[pl.BlockSpec((1, tm, D_MODEL), lambda b, i: (b, i, 0)),
                  _mod_spec(per_batch, 0), _mod_spec(per_batch, 1)] + [_const_spec(w) for w in weights],
        out_specs=[pl.BlockSpec((1, tm, w), lambda b, i: (b, i, 0)) for w in widths],
        out_shape=[jax.ShapeDtypeStruct((B, n, w), f32) for w in widths],
        compiler_params=pltpu.CompilerParams(dimension_semantics=("arbitrary", "arbitrary"),
                                             vmem_limit_bytes=VMEM_LIMIT),
        name=name,
    )(xin, mod, mod, *weights)


def _outproj_call(xin, ys, mod, w_out, ln_g, ln_b, per_batch, tm, name):
    B, n, _ = xin.shape

    def kern(x_ref, ya, yb, yc, yd, g_ref, w_ref, lg_ref, lb_ref, o_ref):
        acc = jnp.zeros((tm, D_MODEL), f32)
        for gi, y_ref in enumerate((ya, yb, yc, yd)):
            acc = acc + jnp.dot(y_ref[0].astype(bf16), w_ref[gi * G:(gi + 1) * G, :], preferred_element_type=f32)
        o_ref[0] = _layer_norm_rows(DEEPNORM_ALPHA * x_ref[0] + g_ref[0] * acc, lg_ref[...], lb_ref[...])

    yspec = pl.BlockSpec((1, tm, G), lambda b, i: (b, i, 0))
    return pl.pallas_call(
        kern,
        grid=(B, n // tm),
        in_specs=[pl.BlockSpec((1, tm, D_MODEL), lambda b, i: (b, i, 0)), yspec, yspec, yspec, yspec,
                  _mod_spec(per_batch, 2), _const_spec(w_out), _const_spec(ln_g), _const_spec(ln_b)],
        out_specs=pl.BlockSpec((1, tm, D_MODEL), lambda b, i: (b, i, 0)),
        out_shape=jax.ShapeDtypeStruct((B, n, D_MODEL), f32),
        compiler_params=pltpu.CompilerParams(dimension_semantics=("arbitrary", "arbitrary"),
                                             vmem_limit_bytes=VMEM_LIMIT),
        name=name,
    )(xin, *ys, mod, w_out, ln_g, ln_b)


def _mlp_call(xin, mod, w1, w2, ln_g, ln_b, per_batch, tm, name):
    B, n, _ = xin.shape
    tf = 1024

    def kern(x_ref, sh_ref, sc_ref, g_ref, w1_ref, w2_ref, lg_ref, lb_ref, o_ref):
        xv = x_ref[0]
        hm = (xv * (1.0 + sc_ref[0]) + sh_ref[0]).astype(bf16)
        acc = jnp.zeros((tm, D_MODEL), f32)
        for j in range(D_FF // tf):
            a = jnp.maximum(jnp.dot(hm, w1_ref[:, j * tf:(j + 1) * tf], preferred_element_type=f32), 0.0)
            acc = acc + jnp.dot((a * a).astype(bf16), w2_ref[j * tf:(j + 1) * tf, :], preferred_element_type=f32)
        o_ref[0] = _layer_norm_rows(DEEPNORM_ALPHA * xv + g_ref[0] * acc, lg_ref[...], lb_ref[...])

    return pl.pallas_call(
        kern,
        grid=(B, n // tm),
        in_specs=[pl.BlockSpec((1, tm, D_MODEL), lambda b, i: (b, i, 0)),
                  _mod_spec(per_batch, 3), _mod_spec(per_batch, 4), _mod_spec(per_batch, 5),
                  _const_spec(w1), _const_spec(w2), _const_spec(ln_g), _const_spec(ln_b)],
        out_specs=pl.BlockSpec((1, tm, D_MODEL), lambda b, i: (b, i, 0)),
        out_shape=jax.ShapeDtypeStruct((B, n, D_MODEL), f32),
        compiler_params=pltpu.CompilerParams(dimension_semantics=("arbitrary", "arbitrary"),
                                             vmem_limit_bytes=VMEM_LIMIT),
        name=name,
    )(xin, mod, mod, mod, w1, w2, ln_g, ln_b)


def _pad_lanes(t, start, width=SCALAR_W):
    flat = t.reshape(1, -1).astype(f32)
    return jnp.pad(flat, ((0, 0), (start, width - start - flat.shape[1])))


def _split_w_in(w):
    oa, ob, oc = W_A, W_A + W_B, W_A + W_B + (3 * G + 4 * H + G)
    wa, wb, wc, wd = w[:, :oa], w[:, oa:ob], w[:, ob:oc], w[:, oc:]

    def relayout(t):
        qkv, scal, gate = t[:, :3 * G], t[:, 3 * G:3 * G + 4 * H], t[:, 3 * G + 4 * H:]
        return jnp.concatenate([qkv, gate, scal, jnp.zeros((t.shape[0], SCALAR_W - 4 * H), t.dtype)], axis=1)

    return [t.astype(bf16) for t in (wa, wb, relayout(wc), relayout(wd))]


def kernel(x, c, ctx, c_ctx, ada_w, ada_b, w_in, w_out, ln1_g, ln1_b, ln2_g, ln2_b, mlp_w1, mlp_w2,
           hgrn_gamma, hgrn_norm_g, rwkv_mu, rwkv_w0, rwkv_w2, rwkv_a0, rwkv_a2, rwkv_g2, rwkv_k_k,
           rwkv_k_a, rwkv_r_k, rwkv_ln_g, rwkv_ln_b, gdn_conv, gdn_a_log, gdn_dt_bias, gdn_norm_g,
           mlstm_i_bias, mlstm_f_bias, mlstm_norm_g):
    B = x.shape[0]
    depth = ada_w.shape[0]
    cc = jnp.concatenate([c, c_ctx[None, :], jnp.zeros((8 - B - 1, D_MODEL), f32)], axis=0)
    mods = _ada_call(cc, ada_w, ada_b)
    lb_cum = jnp.cumsum(jax.nn.softmax(hgrn_gamma.astype(f32), axis=0), axis=0)
    row = lambda t: t.reshape(1, -1).astype(f32)
    state = lambda k: pltpu.VMEM((B * NP, P2, k * P2), f32)
    ext = lambda w: pltpu.VMEM((B, C + 2 * HALO, w), f32)

    for layer in range(depth):
        last = layer == depth - 1
        odd = layer % 2 == 1
        mod_lat = mods[layer, :B].reshape(B, 1, 6 * D_MODEL)
        mod_ctx = mods[layer, B:B + 1].reshape(1, 1, 6 * D_MODEL)
        w_groups = _split_w_in(w_in[layer])
        p_lat = _proj_call(x, mod_lat, w_groups, True, 256, "proj_lat")
        p_ctx = _proj_call(ctx, mod_ctx, w_groups, False, 256, "proj_ctx")

        lb = lb_cum[layer] - lb_cum[0]
        ys = [
            _run_mixer(_hgrn_body, odd, False, p_ctx[0], p_lat[0],
                       [lb, row(hgrn_norm_g[layer])], [state(1)], "hgrn"),
            _run_mixer(_rwkv_body, odd, True, p_ctx[1], p_lat[1],
                       [row(rwkv_mu[layer]), rwkv_w0[layer], rwkv_w2[layer], rwkv_a0[layer], rwkv_a2[layer],
                        rwkv_g2[layer], row(rwkv_k_k[layer]), row(rwkv_k_a[layer]), row(rwkv_r_k[layer]),
                        row(rwkv_ln_g[layer]), row(rwkv_ln_b[layer])],
                       [state(1), ext(W_B)], "rwkv"),
            _run_mixer(_gdn_body, not odd, True, p_ctx[2], p_lat[2],
                       [gdn_conv[layer], _pad_lanes(gdn_a_log[layer], 2 * H), _pad_lanes(gdn_dt_bias[layer], 2 * H),
                        row(gdn_norm_g[layer])],
                       [state(1), ext(3 * G)], "gdn"),
            _run_mixer(_mlstm_body, not odd, False, p_ctx[3], p_lat[3],
                       [_pad_lanes(mlstm_i_bias[layer], 0), _pad_lanes(mlstm_f_bias[layer], 2 * H),
                        row(mlstm_norm_g[layer])],
                       [state(2), pltpu.VMEM((8, SCALAR_W), f32)], "mlstm"),
        ]
        wo = w_out[layer].astype(bf16)
        w1 = mlp_w1[layer].astype(bf16)
        w2 = mlp_w2[layer].astype(bf16)
        g1, b1, g2, b2 = row(ln1_g[layer]), row(ln1_b[layer]), row(ln2_g[layer]), row(ln2_b[layer])
        x = _outproj_call(x, [y[1] for y in ys], mod_lat, wo, g1, b1, True, 256, "outproj_lat")
        x = _mlp_call(x, mod_lat, w1, w2, g2, b2, True, 256, "mlp_lat")
        if not last:
            ctx = _outproj_call(ctx, [y[0] for y in ys], mod_ctx, wo, g1, b1, False, 256, "outproj_ctx")
            ctx = _mlp_call(ctx, mod_ctx, w1, w2, g2, b2, False, 256, "mlp_ctx")
    return x
```

```python
import math

import jax
import jax.numpy as jnp
from jax import lax
from jax.experimental import pallas as pl
from jax.experimental.pallas import tpu as pltpu

f32 = jnp.float32
bf16 = jnp.bfloat16

D_MODEL = 1024
DEPTH = 4
GRID_W = 64
GROUP_W = 256
HEAD_DIM = 64
GROUP_HEADS = 4
D_FF = 4 * D_MODEL
CHUNK = 64
SUB = 16
HALO = 8
RWKV_DECAY_RANK = 64
RWKV_AAA_RANK = 64
RWKV_GATE_RANK = 128
GDN_CONV_W = 5
LN_EPS = 1e-5
RWKV_GN_EPS = 64e-5
NORM_EPS = 1e-6
DEEPNORM_ALPHA = (2.0 * DEPTH) ** 0.25
SCALAR_W = 128
W_A = 5 * GROUP_W
W_B = 3 * GROUP_W + 2 * RWKV_DECAY_RANK + 2 * RWKV_AAA_RANK + RWKV_GATE_RANK
W_CD = 4 * GROUP_W + SCALAR_W
VMEM_LIMIT = 56 * 1024 * 1024

NN = (((1,), (0,)), ((), ()))
NT = (((1,), (1,)), ((), ()))
TN = (((0,), (0,)), ((), ()))

G, H, HD, C = GROUP_W, GROUP_HEADS, HEAD_DIM, CHUNK
P2 = 2 * HD
NP = H // 2
assert HD == C and P2 == 128


def _mm(a, b, dims=NN):
    return lax.dot_general(a, b, dims, preferred_element_type=f32)


def _dot(a, b, dims=NN):
    return _mm(a.astype(bf16), b.astype(bf16), dims)


def _split(x):
    hi = x.astype(bf16)
    lo = (x - hi.astype(f32)).astype(bf16)
    return hi, lo


def _dot3s(a, b, dims=NN):
    return _mm(a[0], b[0], dims) + (_mm(a[0], b[1], dims) + _mm(a[1], b[0], dims))


def _dot3(a, b, dims=NN):
    return _dot3s(_split(a), _split(b), dims)


def _split_exact(x):
    h = x.astype(bf16)
    r = x - h.astype(f32)
    mid = r.astype(bf16)
    lo = (r - mid.astype(f32)).astype(bf16)
    return h, mid, lo


def _dot_exact_lhs(m, x):
    mb = m.astype(bf16)
    h, mid, lo = _split_exact(x)
    return _mm(mb, h) + (_mm(mb, mid) + _mm(mb, lo))


def _dot_exact_rhs(x, m):
    mb = m.astype(bf16)
    h, mid, lo = _split_exact(x)
    return _mm(h, mb) + (_mm(mid, mb) + _mm(lo, mb))


def _head_sum(t):
    r, c = _iota2(G)
    return _dot_exact_rhs(t, _same_block(r, c, HD))


def _expand_heads(t, lane0):
    r = lax.broadcasted_iota(jnp.int32, (SCALAR_W, G), 0)
    c = lax.broadcasted_iota(jnp.int32, (SCALAR_W, G), 1)
    return _dot_exact_rhs(t, r == lane0 + jnp.right_shift(c, int(math.log2(HD))))


def _softplus(x):
    return jnp.maximum(x, 0.0) + jnp.log1p(jnp.exp(-jnp.abs(x)))


def _log_sigmoid(x):
    return -_softplus(-x)


def _sigmoid(x):
    return jax.nn.sigmoid(x)


def _iota2(n):
    return (lax.broadcasted_iota(jnp.int32, (n, n), 0), lax.broadcasted_iota(jnp.int32, (n, n), 1))


def _same_block(r, c, size):
    sh = int(math.log2(size))
    return jnp.right_shift(r, sh) == jnp.right_shift(c, sh)


def _before(d, strict, n=C, block=None):
    r, c = _iota2(n)
    if d == 0:
        m = (c < r) if strict else (c <= r)
    else:
        m = (c > r) if strict else (c >= r)
    if block is not None and block != n:
        m = jnp.logical_and(m, _same_block(r, c, block))
    return m


def _heads3(t):
    return t.reshape(t.shape[0], H, HD)


def _head_l2(t):
    return t * lax.rsqrt(_head_sum(t * t) + 1e-12)


def _head_rms(t, g):
    return t * lax.rsqrt(_head_sum(t * t) * (1.0 / HD) + NORM_EPS) * g


def _pk_iota(n):
    r = lax.broadcasted_iota(jnp.int32, (n, P2), 0)
    c = jnp.bitwise_and(lax.broadcasted_iota(jnp.int32, (n, P2), 1), HD - 1)
    return r, c


def _pk_before(d, strict):
    r, c = _pk_iota(C)
    if d == 0:
        return (c < r) if strict else (c <= r)
    return (c > r) if strict else (c >= r)


def _bd(t):
    left = lax.broadcasted_iota(jnp.int32, t.shape, 1) < HD
    return jnp.concatenate([jnp.where(left, t, 0.0), jnp.where(left, 0.0, t)], axis=0).astype(bf16)


def _bd_mask(t):
    r = jnp.right_shift(lax.broadcasted_iota(jnp.int32, t.shape, 0), int(math.log2(HD)))
    c = jnp.bitwise_and(jnp.right_shift(lax.broadcasted_iota(jnp.int32, t.shape, 1), int(math.log2(HD))), 1)
    return jnp.where(r == c, t, 0.0)


def _tri_inverse_many(Ls):
    r, c = _pk_iota(C)
    eye = (r == c).astype(f32)
    blk8 = _same_block(r, c, 8)
    ps = [-jnp.where(blk8, L, 0.0) for L in Ls]
    xs = [eye + p for p in ps]
    for _ in range(2):
        ps = [_mm(p.astype(bf16), _bd(p)) for p in ps]
        yield
        xs = [x + _mm(x.astype(bf16), _bd(p)) for x, p in zip(xs, ps)]
        yield
    for s in (8, 16, 32):
        ring = jnp.logical_and(_same_block(r, c, 2 * s), jnp.logical_not(_same_block(r, c, s)))
        ts = [_mm(x.astype(bf16), _bd(jnp.where(ring, L, 0.0))) for x, L in zip(xs, Ls)]
        yield
        xs = [x - _mm(t.astype(bf16), _bd(x)) for x, t in zip(xs, ts)]
        yield
    return xs


def _interleave(gens):
    results = [None] * len(gens)
    active = list(range(len(gens)))
    while active:
        for i in list(active):
            try:
                next(gens[i])
            except StopIteration as stop:
                results[i] = stop.value
                active.remove(i)
    return results


def _layer_norm_rows(t, g, b):
    mu = jnp.mean(t, -1, keepdims=True)
    tc = t - mu
    var = jnp.mean(tc * tc, -1, keepdims=True)
    return tc * lax.rsqrt(var + LN_EPS) * g + b


def _last_rows(t, d, nb, rows):
    return [t[rows * bb + (rows - 1 if d == 0 else 0):rows * bb + (rows if d == 0 else 1)] for bb in range(nb)]


def _instances(nb):
    return [(bb, p) for bb in range(nb) for p in range(NP)]


def _rs(bb):
    return slice(C * bb, C * bb + C)


def _ps(p):
    return slice(P2 * p, P2 * p + P2)


def _pk_rows(rows_t, lane0, bb, p):
    h = lane0 + 2 * p
    return jnp.concatenate([rows_t[h:h + 1, _rs(bb)], rows_t[h + 1:h + 2, _rs(bb)]], axis=1)


def _assemble(o_inst, nb):
    return jnp.concatenate([jnp.concatenate(o_inst[bb * NP:(bb + 1) * NP], axis=1) for bb in range(nb)], axis=0)


def _hgrn_body(d, x, prev8, next8, of, first, lb_ref, ng_ref, s_ref):
    del prev8, next8
    nb = x.shape[0]
    inst = _instances(nb)

    @pl.when(first)
    def _():
        s_ref[...] = jnp.zeros(s_ref.shape, f32)

    xf = x.reshape(nb * C, x.shape[2])
    q = xf[:, 0:G]
    v = xf[:, G:2 * G]
    f = xf[:, (2 + d) * G:(3 + d) * G]
    g = xf[:, 4 * G:5 * G]
    lb = lb_ref[d:d + 1, :]
    a = jnp.log(lb)
    bb_ = jnp.log1p(-lb) + _log_sigmoid(f)
    log_f = jnp.maximum(a, bb_) + jnp.log1p(jnp.exp(-jnp.abs(a - bb_)))
    k = (1.0 - lb) * _sigmoid(-f)

    ns = nb * SUB
    m16 = _before(d, False, ns, SUB).astype(f32)
    lr, lc_ = _iota2(G)
    seg = _same_block(lr, lc_, HD).astype(bf16)
    ri = lax.broadcasted_iota(jnp.int32, (nb, SUB, G), 1)
    n_sub = C // SUB
    o_sub = [None] * n_sub
    for sbi in range(n_sub):
        sb = sbi if d == 0 else n_sub - 1 - sbi
        pick = lambda t: jnp.concatenate([t[C * b + SUB * sb:C * b + SUB * sb + SUB] for b in range(nb)], axis=0)
        lf_s, q_s, k_s, v_s = pick(log_f), pick(q), pick(k), pick(v)
        cum = _dot_exact_lhs(m16, lf_s)
        cum3, q3, k3, v3 = (t.reshape(nb, SUB, G) for t in (cum, q_s, k_s, v_s))
        tot3 = cum3[:, SUB - 1:SUB, :] if d == 0 else cum3[:, 0:1, :]
        qe3 = q3 * jnp.exp(cum3)
        ke3 = k3 * jnp.exp(tot3 - cum3)
        ts = []
        for j in range(SUB):
            e = jnp.exp(jnp.minimum(cum3 - cum3[:, j:j + 1, :], 0.0))
            valid = (ri >= j) if d == 0 else (ri <= j)
            ts.append(jnp.where(valid, q3 * e * k3[:, j:j + 1, :], 0.0).astype(bf16).reshape(ns, G))
        yield
        rsum = _mm(jnp.concatenate(ts, axis=0), seg)
        yield
        od = jnp.zeros((nb, SUB, G), f32)
        for j in range(SUB):
            od = od + rsum[j * ns:(j + 1) * ns].reshape(nb, SUB, G) * v3[:, j:j + 1, :]
        ss = [s_ref[b * NP + p] for b, p in inst]
        oi = [_dot(qe3[b][:, _ps(p)], s, NT) for (b, p), s in zip(inst, ss)]
        for (b, p), s in zip(inst, ss):
            s_ref[b * NP + p] = (s * jnp.exp(tot3[b][:, _ps(p)])
                                 + _bd_mask(_dot(v3[b][:, _ps(p)], ke3[b][:, _ps(p)], TN)))
        o_sub[sb] = [jnp.concatenate(oi[b * NP:(b + 1) * NP], axis=1) + od[b] for b in range(nb)]
        yield
    o = jnp.concatenate([o_sub[sb][b] for b in range(nb) for sb in range(n_sub)], axis=0)
    if d == 0:
        return o
    return _head_rms(o + of, ng_ref[...]) * (g * _sigmoid(g))


def _gdn_body(d, x, prev8, next8, of, first, cw_ref, alog_ref, dtb_ref, ng_ref, s_ref, ext_ref):
    nb = x.shape[0]
    inst = _instances(nb)

    @pl.when(first)
    def _():
        s_ref[...] = jnp.zeros(s_ref.shape, f32)

    wq = 3 * G
    accs = []
    for b in range(nb):
        ext_ref[b, 0:HALO, :] = prev8[b][:, :wq]
        ext_ref[b, HALO:HALO + C, :] = x[b][:, :wq]
        ext_ref[b, HALO + C:2 * HALO + C, :] = next8[b][:, :wq]
        acc = jnp.zeros((C, wq), f32)
        for kk in range(GDN_CONV_W):
            acc = acc + cw_ref[kk:kk + 1, :] * ext_ref[b, pl.ds(HALO - GDN_CONV_W // 2 + kk, C), :]
        accs.append(acc)
    acc = jnp.concatenate(accs, axis=0)
    xf = x.reshape(nb * C, x.shape[2])
    qkv = acc * _sigmoid(acc)
    q = _head_l2(qkv[:, 0:G]) * HD ** -0.5
    k = _head_l2(qkv[:, G:2 * G])
    v = qkv[:, 2 * G:3 * G]
    g = xf[:, 3 * G:4 * G]
    sc = xf[:, 4 * G:4 * G + SCALAR_W]
    beta_all = _sigmoid(sc)
    la_all = -jnp.exp(alog_ref[...]) * _softplus(sc + dtb_ref[...])
    incl = _pk_before(d, False)
    strict = _pk_before(d, True)
    cum_all = _dot_exact_lhs(_before(d, False, nb * C, C).astype(f32), la_all)
    cum_rows = cum_all.T
    last_bc = jnp.concatenate([jnp.broadcast_to(t, (C, SCALAR_W)) for t in _last_rows(cum_all, d, nb, C)], axis=0)
    lb_, la_ = 4 * d, 8 + 4 * d
    nr = nb * C
    beta_e = _expand_heads(beta_all, lb_)
    cum_ld = _expand_heads(jnp.concatenate([cum_all, last_bc - cum_all], axis=0), la_)
    cum_e = cum_ld[:nr]
    ec_e = jnp.exp(cum_e)
    kb_f = k * beta_e
    vb_f = v * beta_e
    kbe_f = kb_f * ec_e
    qe_f = q * ec_e
    kl_f = k * jnp.exp(cum_ld[nr:])
    sl = lambda t: [t[_rs(b), _ps(p)] for b, p in inst]
    qh, kh, kb, vb, kbe, qe, kl, cc = sl(q), sl(k), sl(kb_f), sl(vb_f), sl(kbe_f), sl(qe_f), sl(kl_f), sl(cum_e)
    cr = [_pk_rows(cum_rows, la_, b, p) for b, p in inst]
    row_last = C - 1 if d == 0 else 0
    e_last = [ec_e[C * b + row_last:C * b + row_last + 1, _ps(p)] for b, p in inst]
    yield
    gam = [jnp.where(incl, jnp.exp(jnp.minimum(c_ - r_, 0.0)), 0.0) for c_, r_ in zip(cc, cr)]
    kk_qk = [_mm(jnp.concatenate([kb_, q_], axis=0).astype(bf16), _bd(k_), NT) * jnp.concatenate([g_, g_], axis=0)
             for kb_, q_, k_, g_ in zip(kb, qh, kh, gam)]
    yield
    Ts = yield from _tri_inverse_many([jnp.where(strict, a_[:C], 0.0) for a_ in kk_qk])
    sol = [_mm(T.astype(bf16), jnp.concatenate([_bd(vb_), _bd(kbe_)], axis=1)) for T, vb_, kbe_ in zip(Ts, vb, kbe)]
    yield
    ss = [s_ref[b * NP + p] for b, p in inst]
    ws_qs = [_dot(jnp.concatenate([so[:, P2:], qe_], axis=0), s) for so, qe_, s in zip(sol, qe, ss)]
    yield
    vn = [so[:, :P2] - t_[:C] for so, t_ in zip(sol, ws_qs)]
    o_inst = [t_[C:] + _mm(a_[C:].astype(bf16), _bd(vn_)) for t_, a_, vn_ in zip(ws_qs, kk_qk, vn)]
    yield
    for (b, p), s, el_, kl_, vn_ in zip(inst, ss, e_last, kl, vn):
        s_ref[b * NP + p] = el_ * s + _bd_mask(_dot(kl_, vn_, TN))
    o = _assemble(o_inst, nb)
    if d == 0:
        return o
    return _head_rms(o + of, ng_ref[...]) * (g * _sigmoid(g))


def _mlstm_body(d, x, prev8, next8, of, first, ib_ref, fb_ref, ng_ref, c_ref, m_ref):
    del prev8, next8
    nb = x.shape[0]
    inst = _instances(nb)

    @pl.when(first)
    def _():
        c_ref[...] = jnp.zeros(c_ref.shape, f32)
        m_ref[...] = jnp.zeros(m_ref.shape, f32)

    xf = x.reshape(nb * C, x.shape[2])
    q = xf[:, 0:G]
    k = xf[:, G:2 * G] * HD ** -0.5
    v = xf[:, 2 * G:3 * G]
    og = xf[:, 3 * G:4 * G]
    sc = xf[:, 4 * G:4 * G + SCALAR_W]
    li_all = sc + ib_ref[...]
    lf_all = _log_sigmoid(sc + fb_ref[...])
    incl = _pk_before(d, False)
    nr = nb * C
    stack = lambda rows: jnp.concatenate([jnp.broadcast_to(t, (C, SCALAR_W)) for t in rows], axis=0)
    b_all = _dot_exact_lhs(_before(d, False, nr, C).astype(f32), lf_all)
    li_sh = pltpu.roll(li_all, 2 * H, axis=1)
    m_rows = [m_ref[b:b + 1, :] for b in range(nb)]
    m_all = stack(m_rows)
    a_all = li_sh - b_all
    loc = jnp.bitwise_and(lax.broadcasted_iota(jnp.int32, (nr, SCALAR_W), 0), C - 1)
    run = a_all
    sh = 1
    while sh < C:
        if d == 0:
            shifted = jnp.where(loc >= sh, pltpu.roll(run, sh, axis=0), -jnp.inf)
        else:
            shifted = jnp.where(loc < C - sh, pltpu.roll(run, nr - sh, axis=0), -jnp.inf)
        run = jnp.maximum(run, shifted)
        sh *= 2
    mx_all = jnp.maximum(run, m_all)
    iw_all = jnp.exp(m_all - mx_all)
    eneg_all = jnp.exp(-(b_all + mx_all))
    bl_rows = _last_rows(b_all, d, nb, C)
    src_all = stack(bl_rows) - b_all + li_sh
    mnew_rows = [jnp.maximum(bl_ + m_, jnp.max(src_all[_rs(b)], 0, keepdims=True))
                 for b, (bl_, m_) in enumerate(zip(bl_rows, m_rows))]
    cw_rows = [jnp.exp(bl_ + m_ - mn_) for bl_, m_, mn_ in zip(bl_rows, m_rows, mnew_rows)]
    wj_all = jnp.exp(src_all - stack(mnew_rows))
    a_rows = a_all.T
    ln = 2 * H + 4 * d
    ex = _expand_heads(jnp.concatenate([mx_all, iw_all, eneg_all, wj_all, stack(cw_rows)], axis=0), ln)
    mx_e, iw_e, en_e, wj_e, cw_e = (ex[i * nr:(i + 1) * nr] for i in range(5))
    sl = lambda t: [t[_rs(b), _ps(p)] for b, p in inst]
    ones = jnp.ones((C, P2), f32)
    bd_ones = _bd(ones)
    qh, qi, kw, mxc, enc = sl(q), sl(q * iw_e), sl(k * wj_e), sl(mx_e), sl(en_e)
    kh, vh = sl(k), sl(v)
    arow = [_pk_rows(a_rows, ln, b, p) for b, p in inst]
    cw = [cw_e[C * b:C * b + 1, _ps(p)] for b, p in inst]
    yield
    dw = [jnp.where(incl, jnp.exp(jnp.minimum(ar_ - mc_, 0.0)), 0.0) for ar_, mc_ in zip(arow, mxc)]
    s_ = [_mm(q_.astype(bf16), _bd(k_), NT) * dw_ for q_, k_, dw_ in zip(qh, kh, dw)]
    yield
    cst = [c_ref[b * NP + p] for b, p in inst]
    num = [_dot(qi_, c_) + _mm(s__.astype(bf16), jnp.concatenate([_bd(v_), bd_ones], axis=1))
           for qi_, c_, s__, v_ in zip(qi, cst, s_, vh)]
    yield
    o_inst = [n_[:, :P2] * (1.0 / jnp.maximum(jnp.abs(n_[:, P2:]), en_)) for n_, en_ in zip(num, enc)]
    for (b, p), c_, cw_, kw_, v_ in zip(inst, cst, cw, kw, vh):
        c_ref[b * NP + p] = (jnp.concatenate([cw_, cw_], axis=1) * c_
                             + _bd_mask(_dot(kw_, jnp.concatenate([v_, ones], axis=1), TN)))
    for b in range(nb):
        m_ref[b:b + 1, :] = mnew_rows[b]
    o = _assemble(o_inst, nb)
    if d == 0:
        return o
    return _head_rms(o + of, ng_ref[...]) * _sigmoid(og)


def _rwkv_body(d, x, prev8, next8, of, first, mu_ref, w0_ref, w2_ref, a0_ref, a2_ref, g2_ref,
               kk_ref, ka_ref, rk_ref, lng_ref, lnb_ref, s_ref, ext_ref):
    nb = x.shape[0]
    inst = _instances(nb)

    @pl.when(first)
    def _():
        s_ref[...] = jnp.zeros(s_ref.shape, f32)

    nbs = []
    for b in range(nb):
        ext_ref[b, 0:HALO, :] = prev8[b]
        ext_ref[b, HALO:HALO + C, :] = x[b]
        ext_ref[b, HALO + C:2 * HALO + C, :] = next8[b]
        nbs.append(ext_ref[b, pl.ds(HALO - 1, C), :] + ext_ref[b, pl.ds(HALO + 1, C), :])
    xf = x.reshape(nb * C, x.shape[2])
    xs = xf + mu_ref[...] * (0.5 * jnp.concatenate(nbs, axis=0) - xf)
    r = xs[:, 0:G]
    k = xs[:, G:2 * G]
    v = xs[:, 2 * G:3 * G]
    o_w = 3 * G
    o_a = o_w + 2 * RWKV_DECAY_RANK
    o_g = o_a + 2 * RWKV_AAA_RANK

    def a_of(dd):
        ad = xs[:, o_a + RWKV_AAA_RANK * dd:o_a + RWKV_AAA_RANK * (dd + 1)]
        return _sigmoid(a0_ref[dd:dd + 1, :] + _dot3(ad, a2_ref[dd]))

    wd = xs[:, o_w + RWKV_DECAY_RANK * d:o_w + RWKV_DECAY_RANK * (d + 1)]
    w = -_softplus(-(w0_ref[d:d + 1, :] + _dot3(jnp.tanh(wd), w2_ref[d]))) - 0.5
    logw = -jnp.exp(w)
    a = a_of(d)
    kk = _head_l2(k * kk_ref[...])
    kd = k * (1.0 + (a - 1.0) * ka_ref[...])
    beta = kk * a
    incl = _pk_before(d, False)
    strict = _pk_before(d, True)
    lc = _dot_exact_lhs(_before(d, False, nb * C, C).astype(f32), logw)
    tot_all = _last_rows(lc, d, nb, C)
    tot = jnp.concatenate([jnp.broadcast_to(t, (C, G)) for t in tot_all], axis=0)
    e_neg = jnp.exp(-lc)
    e_tot = jnp.exp(tot - lc)
    kp = kk * jnp.exp(lc - logw)
    rp = r * jnp.exp(lc)
    bn = beta * e_neg
    kn = kd * e_neg
    ke = kd * e_tot
    be = beta * e_tot
    sl = lambda t: [t[_rs(b), _ps(p)] for b, p in inst]
    kp_, rp_, bn_, kn_, ke_, be_, vh = sl(kp), sl(rp), sl(bn), sl(kn), sl(ke), sl(be), sl(v)
    ss = [s_ref[b * NP + p] for b, p in inst]
    lhs = [jnp.concatenate([p_, t], axis=0).astype(bf16) for p_, t in zip(kp_, rp_)]
    yield
    quad = [_mm(l_, jnp.concatenate([_bd(b_), _bd(k_)], axis=0), NT) for l_, b_, k_ in zip(lhs, bn_, kn_)]
    ps_rs = [_mm(l_, s.astype(bf16), NT) for l_, s in zip(lhs, ss)]
    yield
    Ts = yield from _tri_inverse_many([jnp.where(strict, q_[:C, :P2], 0.0) for q_ in quad])
    rhs = [t_[:C] + _mm(jnp.where(strict, q_[:C, P2:], 0.0).astype(bf16), _bd(v_))
           for t_, q_, v_ in zip(ps_rs, quad, vh)]
    yield
    u = [_mm(T.astype(bf16), _bd(r_)) for T, r_ in zip(Ts, rhs)]
    yield
    o_inst = [t_[C:] + _mm(jnp.concatenate([jnp.where(incl, q_[C:, P2:], 0.0), jnp.where(incl, -q_[C:, :P2], 0.0)],
                                           axis=1).astype(bf16),
                           jnp.concatenate([_bd(v_), _bd(u_)], axis=0))
              for t_, q_, v_, u_ in zip(ps_rs, quad, vh, u)]
    yield
    for (b, p), s, v_, ke__, u_, be__ in zip(inst, ss, vh, ke_, u, be_):
        s_ref[b * NP + p] = (s * jnp.exp(tot_all[b][:, _ps(p)])
                             + _bd_mask(_dot(jnp.concatenate([v_, -u_], axis=0),
                                             jnp.concatenate([ke__, be__], axis=0), TN)))
    o = _assemble(o_inst, nb)
    if d == 0:
        return o
    kd_other = k * (1.0 + (a_of(1 - d) - 1.0) * ka_ref[...])
    bonus = _head_sum(r * (kd + kd_other) * rk_ref[...]) * v
    t = o + of
    tc = t - _head_sum(t) * (1.0 / HD)
    var = _head_sum(tc * tc) * (1.0 / HD)
    gn = tc * lax.rsqrt(var + RWKV_GN_EPS) * lng_ref[...] + lnb_ref[...]
    gate = _dot(_sigmoid(xs[:, o_g:o_g + RWKV_GATE_RANK]), g2_ref[...])
    return (gn + bonus) * gate


def _mixer_io(m, d, of_ctx, of_lat):
    p_ctx, p_lat, colmajor, use_halo, params = m["p_ctx"], m["p_lat"], m["colmajor"], m["use_halo"], m["params"]
    B, n_ctx, W = p_ctx.shape
    n_lat = p_lat.shape[1]
    nc, nl = n_ctx // C, n_lat // C
    if colmajor:
        assert n_lat == GRID_W * C
    have_of = d == 1

    def cj(s):
        return jnp.clip(s if d == 0 else nc - 1 - s, 0, nc - 1)

    def lj(s):
        return jnp.clip((s - nc) if d == 0 else nl - 1 - (s - nc), 0, nl - 1)

    def lat_view(t, w):
        return t.reshape(B, C, GRID_W * w) if colmajor else t

    def lat_spec(w):
        if colmajor:
            return pl.BlockSpec((B, C, w), lambda s: (0, 0, lj(s)))
        return pl.BlockSpec((B, C, w), lambda s: (0, lj(s), 0))

    def ctx_spec(w):
        return pl.BlockSpec((B, C, w), lambda s: (0, cj(s), 0))

    rpc = C // HALO
    operands = [p_ctx, lat_view(p_lat, W)]
    in_specs = [ctx_spec(W), lat_spec(W)]
    if use_halo:
        operands += [p_ctx, p_ctx, lat_view(p_lat, W), lat_view(p_lat, W)]
        in_specs += [
            pl.BlockSpec((B, HALO, W), lambda s: (0, jnp.maximum(rpc * cj(s) - 1, 0), 0)),
            pl.BlockSpec((B, HALO, W), lambda s: (0, jnp.minimum(rpc * (cj(s) + 1), rpc * nc - 1), 0)),
        ]
        if colmajor:
            in_specs += [
                pl.BlockSpec((B, HALO, W), lambda s: (0, rpc - 1, jnp.maximum(lj(s) - 1, 0))),
                pl.BlockSpec((B, HALO, W), lambda s: (0, 0, jnp.minimum(lj(s) + 1, nl - 1))),
            ]
        else:
            in_specs += [
                pl.BlockSpec((B, HALO, W), lambda s: (0, jnp.maximum(rpc * lj(s) - 1, 0), 0)),
                pl.BlockSpec((B, HALO, W), lambda s: (0, jnp.minimum(rpc * (lj(s) + 1), rpc * nl - 1), 0)),
            ]
    if have_of:
        operands += [of_ctx, lat_view(of_lat, G)]
        in_specs += [ctx_spec(G), lat_spec(G)]
    n_data = len(operands)
    for p in params:
        operands.append(p)
        in_specs.append(pl.BlockSpec(p.shape, lambda s, _n=p.ndim: (0,) * _n))

    def start(refs, scr, s):
        data, par = refs[:n_data], refs[n_data:]
        is_ctx = s < nc
        pick = lambda a, b: jnp.where(is_ctx, a[...], b[...])
        x = pick(data[0], data[1])
        i = 2
        prev8 = next8 = None
        if use_halo:
            j = jnp.where(is_ctx, cj(s), lj(s))
            jmax = jnp.where(is_ctx, nc - 1, nl - 1)
            prev8 = jnp.where(j > 0, pick(data[2], data[4]), 0.0)
            next8 = jnp.where(j < jmax, pick(data[3], data[5]), 0.0)
            i = 6
        of = pick(data[i], data[i + 1]).reshape(B * C, G) if have_of else None
        return m["body"](d, x, prev8, next8, of, s == 0, *par, *scr)

    return dict(operands=operands, in_specs=in_specs, out_specs=[ctx_spec(G), lat_spec(G)],
                out_shape=[jax.ShapeDtypeStruct((B, n_ctx, G), f32),
                           jax.ShapeDtypeStruct((B, C, GRID_W * G) if colmajor else (B, n_lat, G), f32)],
                start=start, steps=nc + nl, nc=nc, lat_shape=(B, n_lat, G))


def _mixer_group_call(mixers, d, ofs, name):
    ios = [_mixer_io(m, d, *of) for m, of in zip(mixers, ofs)]
    steps, nc = ios[0]["steps"], ios[0]["nc"]
    n_in = [len(io["operands"]) for io in ios]
    n_scr = [len(m["scratch"]) for m in mixers]

    def kern(*refs):
        ins, pos = [], 0
        for n in n_in:
            ins.append(refs[pos:pos + n])
            pos += n
        outs = [refs[pos + 2 * i:pos + 2 * i + 2] for i in range(len(ios))]
        pos += 2 * len(ios)
        scrs = []
        for n in n_scr:
            scrs.append(refs[pos:pos + n])
            pos += n
        s = pl.program_id(0)
        is_ctx = s < nc
        ys = _interleave([io["start"](i_, scr, s) for io, i_, scr in zip(ios, ins, scrs)])
        for (oc_ref, ol_ref), y in zip(outs, ys):
            y3 = y.reshape(oc_ref.shape)

            @pl.when(is_ctx)
            def _():
                oc_ref[...] = y3

            @pl.when(jnp.logical_not(is_ctx))
            def _():
                ol_ref[...] = y3

    res = pl.pallas_call(
        kern,
        grid=(steps,),
        in_specs=[sp for io in ios for sp in io["in_specs"]],
        out_specs=[sp for io in ios for sp in io["out_specs"]],
        out_shape=[sh for io in ios for sh in io["out_shape"]],
        scratch_shapes=[sc for m in mixers for sc in m["scratch"]],
        compiler_params=pltpu.CompilerParams(dimension_semantics=("arbitrary",),
                                             vmem_limit_bytes=VMEM_LIMIT),
        name=name,
    )(*[op for io in ios for op in io["operands"]])
    return [(res[2 * i], res[2 * i + 1].reshape(io["lat_shape"])) for i, io in enumerate(ios)]


def _run_mixers(mixers, name):
    fwd = _mixer_group_call(mixers, 0, [(None, None)] * len(mixers), name + "_fwd")
    return _mixer_group_call(mixers, 1, fwd, name + "_bwd")


def _run_mixer(body, colmajor, use_halo, p_ctx, p_lat, params, scratch, name):
    m = dict(body=body, colmajor=colmajor, use_halo=use_halo, p_ctx=p_ctx, p_lat=p_lat, params=params, scratch=scratch)
    return _run_mixers([m], name)[0]


def _ada_call(cc, ada_w, ada_b):
    L, _, n6 = ada_w.shape
    tn = 1536

    def kern(c_ref, w_ref, b_ref, o_ref):
        cv = c_ref[...]
        o_ref[0] = _dot3(cv * _sigmoid(cv), w_ref[0]) + b_ref[0]

    return pl.pallas_call(
        kern,
        grid=(L, n6 // tn),
        in_specs=[pl.BlockSpec(cc.shape, lambda l, j: (0, 0)),
                  pl.BlockSpec((1, D_MODEL, tn), lambda l, j: (l, 0, j)),
                  pl.BlockSpec((1, 1, tn), lambda l, j: (l, 0, j))],
        out_specs=pl.BlockSpec((1, cc.shape[0], tn), lambda l, j: (l, 0, j)),
        out_shape=jax.ShapeDtypeStruct((L, cc.shape[0], n6), f32),
        compiler_params=pltpu.CompilerParams(dimension_semantics=("arbitrary", "arbitrary"),
                                             vmem_limit_bytes=VMEM_LIMIT),
        name="ada",
    )(cc, ada_w, ada_b.reshape(L, 1, n6))


def _mod_spec(per_batch, k):
    if per_batch:
        return pl.BlockSpec((1, 1, D_MODEL), lambda b, i: (b, 0, k))
    return pl.BlockSpec((1, 1, D_MODEL), lambda b, i: (0, 0, k))


def _const_spec(a):
    return pl.BlockSpec(a.shape, lambda b, i, _n=a.ndim: (0,) * _n)


def _proj_call(xin, mod, weights, per_batch, tm, name):
    B, n, _ = xin.shape
    widths = [w.shape[1] for w in weights]

    def kern(x_ref, sh_ref, sc_ref, *rest):
        w_refs, o_refs = rest[:len(weights)], rest[len(weights):]
        hm = (x_ref[0] * (1.0 + sc_ref[0]) + sh_ref[0]).astype(bf16)
        for w_ref, o_ref in zip(w_refs, o_refs):
            o_ref[0] = jnp.dot(hm, w_ref[...], preferred_element_type=f32)

    return pl.pallas_call(
        kern,
        grid=(B, n // tm),
        in_specs=[pl.BlockSpec((1, tm, D_MODEL), lambda b, i: (b, i, 0)),
                  _mod_spec(per_batch, 0), _mod_spec(per_batch, 1)] + [_const_spec(w) for w in weights],
        out_specs=[pl.BlockSpec((1, tm, w), lambda b, i: (b, i, 0)) for w in widths],
        out_shape=[jax.ShapeDtypeStruct((B, n, w), f32) for w in widths],
        compiler_params=pltpu.CompilerParams(dimension_semantics=("arbitrary", "arbitrary"),
                                             vmem_limit_bytes=VMEM_LIMIT),
        name=name,
    )(xin, mod, mod, *weights)


def _outproj_call(xin, ys, mod, w_out, ln_g, ln_b, per_batch, tm, name):
    B, n, _ = xin.shape

    def kern(x_ref, ya, yb, yc, yd, g_ref, w_ref, lg_ref, lb_ref, o_ref):
        acc = jnp.zeros((tm, D_MODEL), f32)
        for gi, y_ref in enumerate((ya, yb, yc, yd)):
            acc = acc + jnp.dot(y_ref[0].astype(bf16), w_ref[gi * G:(gi + 1) * G, :], preferred_element_type=f32)
        o_ref[0] = _layer_norm_rows(DEEPNORM_ALPHA * x_ref[0] + g_ref[0] * acc, lg_ref[...], lb_ref[...])

    yspec = pl.BlockSpec((1, tm, G), lambda b, i: (b, i, 0))
    return pl.pallas_call(
        kern,
        grid=(B, n // tm),
        in_specs=[pl.BlockSpec((1, tm, D_MODEL), lambda b, i: (b, i, 0)), yspec, yspec, yspec, yspec,
                  _mod_spec(per_batch, 2), _const_spec(w_out), _const_spec(ln_g), _const_spec(ln_b)],
        out_specs=pl.BlockSpec((1, tm, D_MODEL), lambda b, i: (b, i, 0)),
        out_shape=jax.ShapeDtypeStruct((B, n, D_MODEL), f32),
        compiler_params=pltpu.CompilerParams(dimension_semantics=("arbitrary", "arbitrary"),
                                             vmem_limit_bytes=VMEM_LIMIT),
        name=name,
    )(xin, *ys, mod, w_out, ln_g, ln_b)


def _mlp_call(xin, mod, w1, w2, ln_g, ln_b, per_batch, tm, name):
    B, n, _ = xin.shape
    tf = 1024

    def kern(x_ref, sh_ref, sc_ref, g_ref, w1_ref, w2_ref, lg_ref, lb_ref, o_ref):
        xv = x_ref[0]
        hm = (xv * (1.0 + sc_ref[0]) + sh_ref[0]).astype(bf16)
        acc = jnp.zeros((tm, D_MODEL), f32)
        for j in range(D_FF // tf):
            a = jnp.maximum(jnp.dot(hm, w1_ref[:, j * tf:(j + 1) * tf], preferred_element_type=f32), 0.0)
            acc = acc + jnp.dot((a * a).astype(bf16), w2_ref[j * tf:(j + 1) * tf, :], preferred_element_type=f32)
        o_ref[0] = _layer_norm_rows(DEEPNORM_ALPHA * xv + g_ref[0] * acc, lg_ref[...], lb_ref[...])

    return pl.pallas_call(
        kern,
        grid=(B, n // tm),
        in_specs=[pl.BlockSpec((1, tm, D_MODEL), lambda b, i: (b, i, 0)),
                  _mod_spec(per_batch, 3), _mod_spec(per_batch, 4), _mod_spec(per_batch, 5),
                  _const_spec(w1), _const_spec(w2), _const_spec(ln_g), _const_spec(ln_b)],
        out_specs=pl.BlockSpec((1, tm, D_MODEL), lambda b, i: (b, i, 0)),
        out_shape=jax.ShapeDtypeStruct((B, n, D_MODEL), f32),
        compiler_params=pltpu.CompilerParams(dimension_semantics=("arbitrary", "arbitrary"),
                                             vmem_limit_bytes=VMEM_LIMIT),
        name=name,
    )(xin, mod, mod, mod, w1, w2, ln_g, ln_b)


def _pad_lanes(t, start, width=SCALAR_W):
    flat = t.reshape(1, -1).astype(f32)
    return jnp.pad(flat, ((0, 0), (start, width - start - flat.shape[1])))


def _split_w_in(w):
    oa, ob, oc = W_A, W_A + W_B, W_A + W_B + (3 * G + 4 * H + G)
    wa, wb, wc, wd = w[:, :oa], w[:, oa:ob], w[:, ob:oc], w[:, oc:]

    def relayout(t):
        qkv, scal, gate = t[:, :3 * G], t[:, 3 * G:3 * G + 4 * H], t[:, 3 * G + 4 * H:]
        return jnp.concatenate([qkv, gate, scal, jnp.zeros((t.shape[0], SCALAR_W - 4 * H), t.dtype)], axis=1)

    return [t.astype(bf16) for t in (wa, wb, relayout(wc), relayout(wd))]


def kernel(x, c, ctx, c_ctx, ada_w, ada_b, w_in, w_out, ln1_g, ln1_b, ln2_g, ln2_b, mlp_w1, mlp_w2,
           hgrn_gamma, hgrn_norm_g, rwkv_mu, rwkv_w0, rwkv_w2, rwkv_a0, rwkv_a2, rwkv_g2, rwkv_k_k,
           rwkv_k_a, rwkv_r_k, rwkv_ln_g, rwkv_ln_b, gdn_conv, gdn_a_log, gdn_dt_bias, gdn_norm_g,
           mlstm_i_bias, mlstm_f_bias, mlstm_norm_g):
    B = x.shape[0]
    depth = ada_w.shape[0]
    cc = jnp.concatenate([c, c_ctx[None, :], jnp.zeros((8 - B - 1, D_MODEL), f32)], axis=0)
    mods = _ada_call(cc, ada_w, ada_b)
    lb_cum = jnp.cumsum(jax.nn.softmax(hgrn_gamma.astype(f32), axis=0), axis=0)
    row = lambda t: t.reshape(1, -1).astype(f32)
    state = lambda k: pltpu.VMEM((B * NP, P2, k * P2), f32)
    ext = lambda w: pltpu.VMEM((B, C + 2 * HALO, w), f32)

    for layer in range(depth):
        last = layer == depth - 1
        odd = layer % 2 == 1
        mod_lat = mods[layer, :B].reshape(B, 1, 6 * D_MODEL)
        mod_ctx = mods[layer, B:B + 1].reshape(1, 1, 6 * D_MODEL)
        w_groups = _split_w_in(w_in[layer])
        p_lat = _proj_call(x, mod_lat, w_groups, True, 256, "proj_lat")
        p_ctx = _proj_call(ctx, mod_ctx, w_groups, False, 256, "proj_ctx")

        lb = lb_cum[layer] - lb_cum[0]
        mixer = lambda body, cm, halo, g, params, scratch: dict(
            body=body, colmajor=cm, use_halo=halo, p_ctx=p_ctx[g], p_lat=p_lat[g], params=params, scratch=scratch)
        hgrn = mixer(_hgrn_body, odd, False, 0, [lb, row(hgrn_norm_g[layer])], [state(1)])
        rwkv = mixer(_rwkv_body, odd, True, 1,
                     [row(rwkv_mu[layer]), rwkv_w0[layer], rwkv_w2[layer], rwkv_a0[layer], rwkv_a2[layer],
                      rwkv_g2[layer], row(rwkv_k_k[layer]), row(rwkv_k_a[layer]), row(rwkv_r_k[layer]),
                      row(rwkv_ln_g[layer]), row(rwkv_ln_b[layer])],
                     [state(1), ext(W_B)])
        gdn = mixer(_gdn_body, not odd, True, 2,
                    [gdn_conv[layer], _pad_lanes(gdn_a_log[layer], 2 * H), _pad_lanes(gdn_dt_bias[layer], 2 * H),
                     row(gdn_norm_g[layer])],
                    [state(1), ext(3 * G)])
        mlstm = mixer(_mlstm_body, not odd, False, 3,
                      [_pad_lanes(mlstm_i_bias[layer], 0), _pad_lanes(mlstm_f_bias[layer], 2 * H),
                       row(mlstm_norm_g[layer])],
                      [state(2), pltpu.VMEM((8, SCALAR_W), f32)])
        ys = _run_mixers([hgrn, rwkv, gdn, mlstm], "mixers")
        wo = w_out[layer].astype(bf16)
        w1 = mlp_w1[layer].astype(bf16)
        w2 = mlp_w2[layer].astype(bf16)
        g1, b1, g2, b2 = row(ln1_g[layer]), row(ln1_b[layer]), row(ln2_g[layer]), row(ln2_b[layer])
        x = _outproj_call(x, [y[1] for y in ys], mod_lat, wo, g1, b1, True, 256, "outproj_lat")
        x = _mlp_call(x, mod_lat, w1, w2, g2, b2, True, 256, "mlp_lat")
        if not last:
            ctx = _outproj_call(ctx, [y[0] for y in ys], mod_ctx, wo, g1, b1, False, 256, "outproj_ctx")
            ctx = _mlp_call(ctx, mod_ctx, w1, w2, g2, b2, False, 256, "mlp_ctx")
    return x
```

```python
import math

import jax
import jax.numpy as jnp
from jax import lax
from jax.experimental import pallas as pl
from jax.experimental.pallas import tpu as pltpu

f32 = jnp.float32
bf16 = jnp.bfloat16

D_MODEL = 1024
DEPTH = 4
GRID_W = 64
GROUP_W = 256
HEAD_DIM = 64
GROUP_HEADS = 4
D_FF = 4 * D_MODEL
CHUNK = 64
SUB = 16
HALO = 8
RWKV_DECAY_RANK = 64
RWKV_AAA_RANK = 64
RWKV_GATE_RANK = 128
GDN_CONV_W = 5
LN_EPS = 1e-5
RWKV_GN_EPS = 64e-5
NORM_EPS = 1e-6
DEEPNORM_ALPHA = (2.0 * DEPTH) ** 0.25
SCALAR_W = 128
W_A = 5 * GROUP_W
W_B = 3 * GROUP_W + 2 * RWKV_DECAY_RANK + 2 * RWKV_AAA_RANK + RWKV_GATE_RANK
W_CD = 4 * GROUP_W + SCALAR_W
VMEM_LIMIT = 56 * 1024 * 1024

NN = (((1,), (0,)), ((), ()))
NT = (((1,), (1,)), ((), ()))
TN = (((0,), (0,)), ((), ()))

G, H, HD, C = GROUP_W, GROUP_HEADS, HEAD_DIM, CHUNK
P2 = 2 * HD
NP = H // 2
assert HD == C and P2 == 128


def _mm(a, b, dims=NN):
    return lax.dot_general(a, b, dims, preferred_element_type=f32)


def _dot(a, b, dims=NN):
    return _mm(a.astype(bf16), b.astype(bf16), dims)


def _split(x):
    hi = x.astype(bf16)
    lo = (x - hi.astype(f32)).astype(bf16)
    return hi, lo


def _dot3s(a, b, dims=NN):
    return _mm(a[0], b[0], dims) + (_mm(a[0], b[1], dims) + _mm(a[1], b[0], dims))


def _dot3(a, b, dims=NN):
    return _dot3s(_split(a), _split(b), dims)


def _split_exact(x):
    h = x.astype(bf16)
    r = x - h.astype(f32)
    mid = r.astype(bf16)
    lo = (r - mid.astype(f32)).astype(bf16)
    return h, mid, lo


def _dot_exact_lhs(m, x):
    mb = m.astype(bf16)
    h, mid, lo = _split_exact(x)
    return _mm(mb, h) + (_mm(mb, mid) + _mm(mb, lo))


def _dot_exact_rhs(x, m):
    mb = m.astype(bf16)
    h, mid, lo = _split_exact(x)
    return _mm(h, mb) + (_mm(mid, mb) + _mm(lo, mb))


def _head_sum(t):
    r, c = _iota2(G)
    return _dot_exact_rhs(t, _same_block(r, c, HD))


def _expand_heads(t, lane0):
    r = lax.broadcasted_iota(jnp.int32, (SCALAR_W, G), 0)
    c = lax.broadcasted_iota(jnp.int32, (SCALAR_W, G), 1)
    return _dot_exact_rhs(t, r == lane0 + jnp.right_shift(c, int(math.log2(HD))))


def _softplus(x):
    return jnp.maximum(x, 0.0) + jnp.log1p(jnp.exp(-jnp.abs(x)))


def _log_sigmoid(x):
    return -_softplus(-x)


def _sigmoid(x):
    return jax.nn.sigmoid(x)


def _iota2(n):
    return (lax.broadcasted_iota(jnp.int32, (n, n), 0), lax.broadcasted_iota(jnp.int32, (n, n), 1))


def _same_block(r, c, size):
    sh = int(math.log2(size))
    return jnp.right_shift(r, sh) == jnp.right_shift(c, sh)


def _before(d, strict, n=C, block=None):
    r, c = _iota2(n)
    if d == 0:
        m = (c < r) if strict else (c <= r)
    else:
        m = (c > r) if strict else (c >= r)
    if block is not None and block != n:
        m = jnp.logical_and(m, _same_block(r, c, block))
    return m


def _heads3(t):
    return t.reshape(t.shape[0], H, HD)


def _head_l2(t):
    return t * lax.rsqrt(_head_sum(t * t) + 1e-12)


def _head_rms(t, g):
    return t * lax.rsqrt(_head_sum(t * t) * (1.0 / HD) + NORM_EPS) * g


def _pk_iota(n):
    r = lax.broadcasted_iota(jnp.int32, (n, P2), 0)
    c = jnp.bitwise_and(lax.broadcasted_iota(jnp.int32, (n, P2), 1), HD - 1)
    return r, c


def _pk_before(d, strict):
    r, c = _pk_iota(C)
    if d == 0:
        return (c < r) if strict else (c <= r)
    return (c > r) if strict else (c >= r)


def _bd(t):
    left = lax.broadcasted_iota(jnp.int32, t.shape, 1) < HD
    return jnp.concatenate([jnp.where(left, t, 0.0), jnp.where(left, 0.0, t)], axis=0).astype(bf16)


def _bd_mask(t):
    r = jnp.right_shift(lax.broadcasted_iota(jnp.int32, t.shape, 0), int(math.log2(HD)))
    c = jnp.bitwise_and(jnp.right_shift(lax.broadcasted_iota(jnp.int32, t.shape, 1), int(math.log2(HD))), 1)
    return jnp.where(r == c, t, 0.0)


def _tri_inverse_many(Ls):
    r, c = _pk_iota(C)
    eye = (r == c).astype(f32)
    blk8 = _same_block(r, c, 8)
    ps = [-jnp.where(blk8, L, 0.0) for L in Ls]
    xs = [eye + p for p in ps]
    for _ in range(2):
        ps = [_mm(p.astype(bf16), _bd(p)) for p in ps]
        yield
        xs = [x + _mm(x.astype(bf16), _bd(p)) for x, p in zip(xs, ps)]
        yield
    for s in (8, 16, 32):
        ring = jnp.logical_and(_same_block(r, c, 2 * s), jnp.logical_not(_same_block(r, c, s)))
        ts = [_mm(x.astype(bf16), _bd(jnp.where(ring, L, 0.0))) for x, L in zip(xs, Ls)]
        yield
        xs = [x - _mm(t.astype(bf16), _bd(x)) for x, t in zip(xs, ts)]
        yield
    return xs


def _interleave(gens):
    results = [None] * len(gens)
    active = list(range(len(gens)))
    while active:
        for i in list(active):
            try:
                next(gens[i])
            except StopIteration as stop:
                results[i] = stop.value
                active.remove(i)
    return results


def _layer_norm_rows(t, g, b):
    mu = jnp.mean(t, -1, keepdims=True)
    tc = t - mu
    var = jnp.mean(tc * tc, -1, keepdims=True)
    return tc * lax.rsqrt(var + LN_EPS) * g + b


def _last_rows(t, d, nb, rows):
    return [t[rows * bb + (rows - 1 if d == 0 else 0):rows * bb + (rows if d == 0 else 1)] for bb in range(nb)]


def _instances(nb):
    return [(bb, p) for bb in range(nb) for p in range(NP)]


def _rs(bb):
    return slice(C * bb, C * bb + C)


def _ps(p):
    return slice(P2 * p, P2 * p + P2)


def _pk_rows(rows_t, lane0, bb, p):
    h = lane0 + 2 * p
    return jnp.concatenate([rows_t[h:h + 1, _rs(bb)], rows_t[h + 1:h + 2, _rs(bb)]], axis=1)


def _assemble(o_inst, nb):
    return jnp.concatenate([jnp.concatenate(o_inst[bb * NP:(bb + 1) * NP], axis=1) for bb in range(nb)], axis=0)


def _hgrn_body(d, x, prev8, next8, of, first, lb_ref, ng_ref, s_ref):
    del prev8, next8
    nb = x.shape[0]
    inst = _instances(nb)

    @pl.when(first)
    def _():
        s_ref[...] = jnp.zeros(s_ref.shape, f32)

    xf = x.reshape(nb * C, x.shape[2])
    q = xf[:, 0:G]
    v = xf[:, G:2 * G]
    f = xf[:, (2 + d) * G:(3 + d) * G]
    g = xf[:, 4 * G:5 * G]
    lb = lb_ref[d:d + 1, :]
    a = jnp.log(lb)
    bb_ = jnp.log1p(-lb) + _log_sigmoid(f)
    log_f = jnp.maximum(a, bb_) + jnp.log1p(jnp.exp(-jnp.abs(a - bb_)))
    k = (1.0 - lb) * _sigmoid(-f)

    ns = nb * SUB
    m16 = _before(d, False, ns, SUB).astype(f32)
    lr, lc_ = _iota2(G)
    seg = _same_block(lr, lc_, HD).astype(bf16)
    ri = lax.broadcasted_iota(jnp.int32, (nb, SUB, G), 1)
    n_sub = C // SUB
    o_sub = [None] * n_sub
    for sbi in range(n_sub):
        sb = sbi if d == 0 else n_sub - 1 - sbi
        pick = lambda t: jnp.concatenate([t[C * b + SUB * sb:C * b + SUB * sb + SUB] for b in range(nb)], axis=0)
        lf_s, q_s, k_s, v_s = pick(log_f), pick(q), pick(k), pick(v)
        cum = _dot_exact_lhs(m16, lf_s)
        cum3, q3, k3, v3 = (t.reshape(nb, SUB, G) for t in (cum, q_s, k_s, v_s))
        tot3 = cum3[:, SUB - 1:SUB, :] if d == 0 else cum3[:, 0:1, :]
        qe3 = q3 * jnp.exp(cum3)
        ke3 = k3 * jnp.exp(tot3 - cum3)
        ts = []
        for j in range(SUB):
            e = jnp.exp(jnp.minimum(cum3 - cum3[:, j:j + 1, :], 0.0))
            valid = (ri >= j) if d == 0 else (ri <= j)
            ts.append(jnp.where(valid, q3 * e * k3[:, j:j + 1, :], 0.0).astype(bf16).reshape(ns, G))
        yield
        rsum = _mm(jnp.concatenate(ts, axis=0), seg)
        yield
        od = jnp.zeros((nb, SUB, G), f32)
        for j in range(SUB):
            od = od + rsum[j * ns:(j + 1) * ns].reshape(nb, SUB, G) * v3[:, j:j + 1, :]
        ss = [s_ref[b * NP + p] for b, p in inst]
        oi = [_dot(qe3[b][:, _ps(p)], s, NT) for (b, p), s in zip(inst, ss)]
        for (b, p), s in zip(inst, ss):
            s_ref[b * NP + p] = (s * jnp.exp(tot3[b][:, _ps(p)])
                                 + _bd_mask(_dot(v3[b][:, _ps(p)], ke3[b][:, _ps(p)], TN)))
        o_sub[sb] = [jnp.concatenate(oi[b * NP:(b + 1) * NP], axis=1) + od[b] for b in range(nb)]
        yield
    o = jnp.concatenate([o_sub[sb][b] for b in range(nb) for sb in range(n_sub)], axis=0)
    if d == 0:
        return o
    return _head_rms(o + of, ng_ref[...]) * (g * _sigmoid(g))


def _gdn_body(d, x, prev8, next8, of, first, cw_ref, alog_ref, dtb_ref, ng_ref, s_ref, ext_ref):
    nb = x.shape[0]
    inst = _instances(nb)

    @pl.when(first)
    def _():
        s_ref[...] = jnp.zeros(s_ref.shape, f32)

    wq = 3 * G
    accs = []
    for b in range(nb):
        ext_ref[b, 0:HALO, :] = prev8[b][:, :wq]
        ext_ref[b, HALO:HALO + C, :] = x[b][:, :wq]
        ext_ref[b, HALO + C:2 * HALO + C, :] = next8[b][:, :wq]
        acc = jnp.zeros((C, wq), f32)
        for kk in range(GDN_CONV_W):
            acc = acc + cw_ref[kk:kk + 1, :] * ext_ref[b, pl.ds(HALO - GDN_CONV_W // 2 + kk, C), :]
        accs.append(acc)
    acc = jnp.concatenate(accs, axis=0)
    xf = x.reshape(nb * C, x.shape[2])
    qkv = acc * _sigmoid(acc)
    q = _head_l2(qkv[:, 0:G]) * HD ** -0.5
    k = _head_l2(qkv[:, G:2 * G])
    v = qkv[:, 2 * G:3 * G]
    g = xf[:, 3 * G:4 * G]
    sc = xf[:, 4 * G:4 * G + SCALAR_W]
    beta_all = _sigmoid(sc)
    la_all = -jnp.exp(alog_ref[...]) * _softplus(sc + dtb_ref[...])
    incl = _pk_before(d, False)
    strict = _pk_before(d, True)
    cum_all = _dot_exact_lhs(_before(d, False, nb * C, C).astype(f32), la_all)
    cum_rows = cum_all.T
    last_bc = jnp.concatenate([jnp.broadcast_to(t, (C, SCALAR_W)) for t in _last_rows(cum_all, d, nb, C)], axis=0)
    lb_, la_ = 4 * d, 8 + 4 * d
    nr = nb * C
    beta_e = _expand_heads(beta_all, lb_)
    cum_ld = _expand_heads(jnp.concatenate([cum_all, last_bc - cum_all], axis=0), la_)
    cum_e = cum_ld[:nr]
    ec_e = jnp.exp(cum_e)
    kb_f = k * beta_e
    vb_f = v * beta_e
    kbe_f = kb_f * ec_e
    qe_f = q * ec_e
    kl_f = k * jnp.exp(cum_ld[nr:])
    sl = lambda t: [t[_rs(b), _ps(p)] for b, p in inst]
    qh, kh, kb, vb, kbe, qe, kl, cc = sl(q), sl(k), sl(kb_f), sl(vb_f), sl(kbe_f), sl(qe_f), sl(kl_f), sl(cum_e)
    cr = [_pk_rows(cum_rows, la_, b, p) for b, p in inst]
    row_last = C - 1 if d == 0 else 0
    e_last = [ec_e[C * b + row_last:C * b + row_last + 1, _ps(p)] for b, p in inst]
    yield
    gam = [jnp.where(incl, jnp.exp(jnp.minimum(c_ - r_, 0.0)), 0.0) for c_, r_ in zip(cc, cr)]
    kk_qk = [_mm(jnp.concatenate([kb_, q_], axis=0).astype(bf16), _bd(k_), NT) * jnp.concatenate([g_, g_], axis=0)
             for kb_, q_, k_, g_ in zip(kb, qh, kh, gam)]
    yield
    Ts = yield from _tri_inverse_many([jnp.where(strict, a_[:C], 0.0) for a_ in kk_qk])
    sol = [_mm(T.astype(bf16), jnp.concatenate([_bd(vb_), _bd(kbe_)], axis=1)) for T, vb_, kbe_ in zip(Ts, vb, kbe)]
    yield
    ss = [s_ref[b * NP + p] for b, p in inst]
    ws_qs = [_dot(jnp.concatenate([so[:, P2:], qe_], axis=0), s) for so, qe_, s in zip(sol, qe, ss)]
    yield
    vn = [so[:, :P2] - t_[:C] for so, t_ in zip(sol, ws_qs)]
    o_inst = [t_[C:] + _mm(a_[C:].astype(bf16), _bd(vn_)) for t_, a_, vn_ in zip(ws_qs, kk_qk, vn)]
    yield
    for (b, p), s, el_, kl_, vn_ in zip(inst, ss, e_last, kl, vn):
        s_ref[b * NP + p] = el_ * s + _bd_mask(_dot(kl_, vn_, TN))
    o = _assemble(o_inst, nb)
    if d == 0:
        return o
    return _head_rms(o + of, ng_ref[...]) * (g * _sigmoid(g))


def _mlstm_body(d, x, prev8, next8, of, first, ib_ref, fb_ref, ng_ref, c_ref, m_ref):
    del prev8, next8
    nb = x.shape[0]
    inst = _instances(nb)

    @pl.when(first)
    def _():
        c_ref[...] = jnp.zeros(c_ref.shape, f32)
        m_ref[...] = jnp.zeros(m_ref.shape, f32)

    xf = x.reshape(nb * C, x.shape[2])
    q = xf[:, 0:G]
    k = xf[:, G:2 * G] * HD ** -0.5
    v = xf[:, 2 * G:3 * G]
    og = xf[:, 3 * G:4 * G]
    sc = xf[:, 4 * G:4 * G + SCALAR_W]
    li_all = sc + ib_ref[...]
    lf_all = _log_sigmoid(sc + fb_ref[...])
    incl = _pk_before(d, False)
    nr = nb * C
    stack = lambda rows: jnp.concatenate([jnp.broadcast_to(t, (C, SCALAR_W)) for t in rows], axis=0)
    b_all = _dot_exact_lhs(_before(d, False, nr, C).astype(f32), lf_all)
    li_sh = pltpu.roll(li_all, 2 * H, axis=1)
    m_rows = [m_ref[b:b + 1, :] for b in range(nb)]
    m_all = stack(m_rows)
    a_all = li_sh - b_all
    loc = jnp.bitwise_and(lax.broadcasted_iota(jnp.int32, (nr, SCALAR_W), 0), C - 1)
    run = a_all
    sh = 1
    while sh < C:
        if d == 0:
            shifted = jnp.where(loc >= sh, pltpu.roll(run, sh, axis=0), -jnp.inf)
        else:
            shifted = jnp.where(loc < C - sh, pltpu.roll(run, nr - sh, axis=0), -jnp.inf)
        run = jnp.maximum(run, shifted)
        sh *= 2
    mx_all = jnp.maximum(run, m_all)
    iw_all = jnp.exp(m_all - mx_all)
    eneg_all = jnp.exp(-(b_all + mx_all))
    bl_rows = _last_rows(b_all, d, nb, C)
    src_all = stack(bl_rows) - b_all + li_sh
    mnew_rows = [jnp.maximum(bl_ + m_, jnp.max(src_all[_rs(b)], 0, keepdims=True))
                 for b, (bl_, m_) in enumerate(zip(bl_rows, m_rows))]
    cw_rows = [jnp.exp(bl_ + m_ - mn_) for bl_, m_, mn_ in zip(bl_rows, m_rows, mnew_rows)]
    wj_all = jnp.exp(src_all - stack(mnew_rows))
    a_rows = a_all.T
    ln = 2 * H + 4 * d
    ex = _expand_heads(jnp.concatenate([mx_all, iw_all, eneg_all, wj_all, stack(cw_rows)], axis=0), ln)
    mx_e, iw_e, en_e, wj_e, cw_e = (ex[i * nr:(i + 1) * nr] for i in range(5))
    sl = lambda t: [t[_rs(b), _ps(p)] for b, p in inst]
    ones = jnp.ones((C, P2), f32)
    bd_ones = _bd(ones)
    qh, qi, kw, mxc, enc = sl(q), sl(q * iw_e), sl(k * wj_e), sl(mx_e), sl(en_e)
    kh, vh = sl(k), sl(v)
    arow = [_pk_rows(a_rows, ln, b, p) for b, p in inst]
    cw = [cw_e[C * b:C * b + 1, _ps(p)] for b, p in inst]
    yield
    dw = [jnp.where(incl, jnp.exp(jnp.minimum(ar_ - mc_, 0.0)), 0.0) for ar_, mc_ in zip(arow, mxc)]
    s_ = [_mm(q_.astype(bf16), _bd(k_), NT) * dw_ for q_, k_, dw_ in zip(qh, kh, dw)]
    yield
    cst = [c_ref[b * NP + p] for b, p in inst]
    num = [_dot(qi_, c_) + _mm(s__.astype(bf16), jnp.concatenate([_bd(v_), bd_ones], axis=1))
           for qi_, c_, s__, v_ in zip(qi, cst, s_, vh)]
    yield
    o_inst = [n_[:, :P2] * (1.0 / jnp.maximum(jnp.abs(n_[:, P2:]), en_)) for n_, en_ in zip(num, enc)]
    for (b, p), c_, cw_, kw_, v_ in zip(inst, cst, cw, kw, vh):
        c_ref[b * NP + p] = (jnp.concatenate([cw_, cw_], axis=1) * c_
                             + _bd_mask(_dot(kw_, jnp.concatenate([v_, ones], axis=1), TN)))
    for b in range(nb):
        m_ref[b:b + 1, :] = mnew_rows[b]
    o = _assemble(o_inst, nb)
    if d == 0:
        return o
    return _head_rms(o + of, ng_ref[...]) * _sigmoid(og)


def _rwkv_body(d, x, prev8, next8, of, first, mu_ref, w0_ref, w2_ref, a0_ref, a2_ref, g2_ref,
               kk_ref, ka_ref, rk_ref, lng_ref, lnb_ref, s_ref, ext_ref):
    nb = x.shape[0]
    inst = _instances(nb)

    @pl.when(first)
    def _():
        s_ref[...] = jnp.zeros(s_ref.shape, f32)

    nbs = []
    for b in range(nb):
        ext_ref[b, 0:HALO, :] = prev8[b]
        ext_ref[b, HALO:HALO + C, :] = x[b]
        ext_ref[b, HALO + C:2 * HALO + C, :] = next8[b]
        nbs.append(ext_ref[b, pl.ds(HALO - 1, C), :] + ext_ref[b, pl.ds(HALO + 1, C), :])
    xf = x.reshape(nb * C, x.shape[2])
    xs = xf + mu_ref[...] * (0.5 * jnp.concatenate(nbs, axis=0) - xf)
    r = xs[:, 0:G]
    k = xs[:, G:2 * G]
    v = xs[:, 2 * G:3 * G]
    o_w = 3 * G
    o_a = o_w + 2 * RWKV_DECAY_RANK
    o_g = o_a + 2 * RWKV_AAA_RANK

    def a_of(dd):
        ad = xs[:, o_a + RWKV_AAA_RANK * dd:o_a + RWKV_AAA_RANK * (dd + 1)]
        return _sigmoid(a0_ref[dd:dd + 1, :] + _dot3(ad, a2_ref[dd]))

    wd = xs[:, o_w + RWKV_DECAY_RANK * d:o_w + RWKV_DECAY_RANK * (d + 1)]
    w = -_softplus(-(w0_ref[d:d + 1, :] + _dot3(jnp.tanh(wd), w2_ref[d]))) - 0.5
    logw = -jnp.exp(w)
    a = a_of(d)
    kk = _head_l2(k * kk_ref[...])
    kd = k * (1.0 + (a - 1.0) * ka_ref[...])
    beta = kk * a
    incl = _pk_before(d, False)
    strict = _pk_before(d, True)
    lc = _dot_exact_lhs(_before(d, False, nb * C, C).astype(f32), logw)
    tot_all = _last_rows(lc, d, nb, C)
    tot = jnp.concatenate([jnp.broadcast_to(t, (C, G)) for t in tot_all], axis=0)
    e_neg = jnp.exp(-lc)
    e_tot = jnp.exp(tot - lc)
    kp = kk * jnp.exp(lc - logw)
    rp = r * jnp.exp(lc)
    bn = beta * e_neg
    kn = kd * e_neg
    ke = kd * e_tot
    be = beta * e_tot
    sl = lambda t: [t[_rs(b), _ps(p)] for b, p in inst]
    kp_, rp_, bn_, kn_, ke_, be_, vh = sl(kp), sl(rp), sl(bn), sl(kn), sl(ke), sl(be), sl(v)
    ss = [s_ref[b * NP + p] for b, p in inst]
    lhs = [jnp.concatenate([p_, t], axis=0).astype(bf16) for p_, t in zip(kp_, rp_)]
    yield
    quad = [_mm(l_, jnp.concatenate([_bd(b_), _bd(k_)], axis=0), NT) for l_, b_, k_ in zip(lhs, bn_, kn_)]
    ps_rs = [_mm(l_, s.astype(bf16), NT) for l_, s in zip(lhs, ss)]
    yield
    Ts = yield from _tri_inverse_many([jnp.where(strict, q_[:C, :P2], 0.0) for q_ in quad])
    rhs = [t_[:C] + _mm(jnp.where(strict, q_[:C, P2:], 0.0).astype(bf16), _bd(v_))
           for t_, q_, v_ in zip(ps_rs, quad, vh)]
    yield
    u = [_mm(T.astype(bf16), _bd(r_)) for T, r_ in zip(Ts, rhs)]
    yield
    o_inst = [t_[C:] + _mm(jnp.concatenate([jnp.where(incl, q_[C:, P2:], 0.0), jnp.where(incl, -q_[C:, :P2], 0.0)],
                                           axis=1).astype(bf16),
                           jnp.concatenate([_bd(v_), _bd(u_)], axis=0))
              for t_, q_, v_, u_ in zip(ps_rs, quad, vh, u)]
    yield
    for (b, p), s, v_, ke__, u_, be__ in zip(inst, ss, vh, ke_, u, be_):
        s_ref[b * NP + p] = (s * jnp.exp(tot_all[b][:, _ps(p)])
                             + _bd_mask(_dot(jnp.concatenate([v_, -u_], axis=0),
                                             jnp.concatenate([ke__, be__], axis=0), TN)))
    o = _assemble(o_inst, nb)
    if d == 0:
        return o
    kd_other = k * (1.0 + (a_of(1 - d) - 1.0) * ka_ref[...])
    bonus = _head_sum(r * (kd + kd_other) * rk_ref[...]) * v
    t = o + of
    tc = t - _head_sum(t) * (1.0 / HD)
    var = _head_sum(tc * tc) * (1.0 / HD)
    gn = tc * lax.rsqrt(var + RWKV_GN_EPS) * lng_ref[...] + lnb_ref[...]
    gate = _dot(_sigmoid(xs[:, o_g:o_g + RWKV_GATE_RANK]), g2_ref[...])
    return (gn + bonus) * gate


def _mixer_io(m, d, of_ctx, of_lat):
    p_ctx, p_lat, colmajor, use_halo, params = m["p_ctx"], m["p_lat"], m["colmajor"], m["use_halo"], m["params"]
    B, n_ctx, W = p_ctx.shape
    if colmajor:
        assert p_lat.shape == (B, C, GRID_W * W)
        n_lat = GRID_W * C
    else:
        n_lat = p_lat.shape[1]
    nc, nl = n_ctx // C, n_lat // C
    have_of = d == 1

    def cj(s):
        return jnp.clip(s if d == 0 else nc - 1 - s, 0, nc - 1)

    def lj(s):
        return jnp.clip((s - nc) if d == 0 else nl - 1 - (s - nc), 0, nl - 1)

    def lat_view(t, w):
        return t

    def lat_spec(w):
        if colmajor:
            return pl.BlockSpec((B, C, w), lambda s: (0, 0, lj(s)))
        return pl.BlockSpec((B, C, w), lambda s: (0, lj(s), 0))

    def ctx_spec(w):
        return pl.BlockSpec((B, C, w), lambda s: (0, cj(s), 0))

    rpc = C // HALO
    operands = [p_ctx, lat_view(p_lat, W)]
    in_specs = [ctx_spec(W), lat_spec(W)]
    if use_halo:
        operands += [p_ctx, p_ctx, lat_view(p_lat, W), lat_view(p_lat, W)]
        in_specs += [
            pl.BlockSpec((B, HALO, W), lambda s: (0, jnp.maximum(rpc * cj(s) - 1, 0), 0)),
            pl.BlockSpec((B, HALO, W), lambda s: (0, jnp.minimum(rpc * (cj(s) + 1), rpc * nc - 1), 0)),
        ]
        if colmajor:
            in_specs += [
                pl.BlockSpec((B, HALO, W), lambda s: (0, rpc - 1, jnp.maximum(lj(s) - 1, 0))),
                pl.BlockSpec((B, HALO, W), lambda s: (0, 0, jnp.minimum(lj(s) + 1, nl - 1))),
            ]
        else:
            in_specs += [
                pl.BlockSpec((B, HALO, W), lambda s: (0, jnp.maximum(rpc * lj(s) - 1, 0), 0)),
                pl.BlockSpec((B, HALO, W), lambda s: (0, jnp.minimum(rpc * (lj(s) + 1), rpc * nl - 1), 0)),
            ]
    if have_of:
        operands += [of_ctx, lat_view(of_lat, G)]
        in_specs += [ctx_spec(G), lat_spec(G)]
    n_data = len(operands)
    for p in params:
        operands.append(p)
        in_specs.append(pl.BlockSpec(p.shape, lambda s, _n=p.ndim: (0,) * _n))

    def start(refs, scr, s):
        data, par = refs[:n_data], refs[n_data:]
        is_ctx = s < nc
        pick = lambda a, b: jnp.where(is_ctx, a[...], b[...])
        x = pick(data[0], data[1])
        i = 2
        prev8 = next8 = None
        if use_halo:
            j = jnp.where(is_ctx, cj(s), lj(s))
            jmax = jnp.where(is_ctx, nc - 1, nl - 1)
            prev8 = jnp.where(j > 0, pick(data[2], data[4]), 0.0)
            next8 = jnp.where(j < jmax, pick(data[3], data[5]), 0.0)
            i = 6
        of = pick(data[i], data[i + 1]).reshape(B * C, G) if have_of else None
        return m["body"](d, x, prev8, next8, of, s == 0, *par, *scr)

    return dict(operands=operands, in_specs=in_specs, out_specs=[ctx_spec(G), lat_spec(G)],
                out_shape=[jax.ShapeDtypeStruct((B, n_ctx, G), f32),
                           jax.ShapeDtypeStruct((B, C, GRID_W * G) if colmajor else (B, n_lat, G), f32)],
                start=start, steps=nc + nl, nc=nc)


def _mixer_group_call(mixers, d, ofs, name):
    ios = [_mixer_io(m, d, *of) for m, of in zip(mixers, ofs)]
    steps, nc = ios[0]["steps"], ios[0]["nc"]
    n_in = [len(io["operands"]) for io in ios]
    n_scr = [len(m["scratch"]) for m in mixers]

    def kern(*refs):
        ins, pos = [], 0
        for n in n_in:
            ins.append(refs[pos:pos + n])
            pos += n
        outs = [refs[pos + 2 * i:pos + 2 * i + 2] for i in range(len(ios))]
        pos += 2 * len(ios)
        scrs = []
        for n in n_scr:
            scrs.append(refs[pos:pos + n])
            pos += n
        s = pl.program_id(0)
        is_ctx = s < nc
        ys = _interleave([io["start"](i_, scr, s) for io, i_, scr in zip(ios, ins, scrs)])
        for (oc_ref, ol_ref), y in zip(outs, ys):
            y3 = y.reshape(oc_ref.shape)

            @pl.when(is_ctx)
            def _():
                oc_ref[...] = y3

            @pl.when(jnp.logical_not(is_ctx))
            def _():
                ol_ref[...] = y3

    res = pl.pallas_call(
        kern,
        grid=(steps,),
        in_specs=[sp for io in ios for sp in io["in_specs"]],
        out_specs=[sp for io in ios for sp in io["out_specs"]],
        out_shape=[sh for io in ios for sh in io["out_shape"]],
        scratch_shapes=[sc for m in mixers for sc in m["scratch"]],
        compiler_params=pltpu.CompilerParams(dimension_semantics=("arbitrary",),
                                             vmem_limit_bytes=VMEM_LIMIT),
        name=name,
    )(*[op for io in ios for op in io["operands"]])
    return [(res[2 * i], res[2 * i + 1]) for i in range(len(ios))]


def _run_mixers(mixers, name):
    fwd = _mixer_group_call(mixers, 0, [(None, None)] * len(mixers), name + "_fwd")
    return _mixer_group_call(mixers, 1, fwd, name + "_bwd")


def _run_mixer(body, colmajor, use_halo, p_ctx, p_lat, params, scratch, name):
    m = dict(body=body, colmajor=colmajor, use_halo=use_halo, p_ctx=p_ctx, p_lat=p_lat, params=params, scratch=scratch)
    return _run_mixers([m], name)[0]


def _ada_call(cc, ada_w, ada_b):
    L, _, n6 = ada_w.shape
    tn = 1536

    def kern(c_ref, w_ref, b_ref, o_ref):
        cv = c_ref[...]
        o_ref[0] = _dot3(cv * _sigmoid(cv), w_ref[0]) + b_ref[0]

    return pl.pallas_call(
        kern,
        grid=(L, n6 // tn),
        in_specs=[pl.BlockSpec(cc.shape, lambda l, j: (0, 0)),
                  pl.BlockSpec((1, D_MODEL, tn), lambda l, j: (l, 0, j)),
                  pl.BlockSpec((1, 1, tn), lambda l, j: (l, 0, j))],
        out_specs=pl.BlockSpec((1, cc.shape[0], tn), lambda l, j: (l, 0, j)),
        out_shape=jax.ShapeDtypeStruct((L, cc.shape[0], n6), f32),
        compiler_params=pltpu.CompilerParams(dimension_semantics=("arbitrary", "arbitrary"),
                                             vmem_limit_bytes=VMEM_LIMIT),
        name="ada",
    )(cc, ada_w, ada_b.reshape(L, 1, n6))


def _mod_spec(per_batch, k):
    if per_batch:
        return pl.BlockSpec((1, 1, D_MODEL), lambda b, i: (b, 0, k))
    return pl.BlockSpec((1, 1, D_MODEL), lambda b, i: (0, 0, k))


def _const_spec(a):
    return pl.BlockSpec(a.shape, lambda b, i, _n=a.ndim: (0,) * _n)


LANES = 128
TM_GRID = HALO * GRID_W


def _proj_call(xin, mod, weights, colmajor, per_batch, tm, name):
    B, n, _ = xin.shape
    widths = [w.shape[1] for w in weights]
    nw = len(weights)
    if any(colmajor):
        assert tm == TM_GRID and n == GRID_W * GRID_W

    def kern(x_ref, sh_ref, sc_ref, *rest):
        w_refs, o_refs, scr = rest[:nw], rest[nw:2 * nw], rest[2 * nw:]
        hm = (x_ref[0] * (1.0 + sc_ref[0]) + sh_ref[0]).astype(bf16)
        bufs = list(scr)
        for w_ref, o_ref, cm, wd in zip(w_refs, o_refs, colmajor, widths):
            res = jnp.dot(hm, w_ref[...], preferred_element_type=f32)
            if not cm:
                o_ref[0] = res
                continue
            buf = bufs.pop(0)
            for k in range(wd // LANES):
                buf[k] = res[:, LANES * k:LANES * (k + 1)]
            for c in range(GRID_W):
                for k in range(wd // LANES):
                    o_ref[0, :, c * wd + LANES * k:c * wd + LANES * (k + 1)] = buf[k, pl.ds(c, HALO, stride=GRID_W), :]

    out_specs = [pl.BlockSpec((1, HALO, GRID_W * w), lambda b, i: (b, i, 0)) if cm
                 else pl.BlockSpec((1, tm, w), lambda b, i: (b, i, 0)) for w, cm in zip(widths, colmajor)]
    out_shape = [jax.ShapeDtypeStruct((B, GRID_W, GRID_W * w) if cm else (B, n, w), f32)
                 for w, cm in zip(widths, colmajor)]
    scratch = [pltpu.VMEM((w // LANES, tm, LANES), f32) for w, cm in zip(widths, colmajor) if cm]
    return pl.pallas_call(
        kern,
        grid=(B, n // tm),
        in_specs=[pl.BlockSpec((1, tm, D_MODEL), lambda b, i: (b, i, 0)),
                  _mod_spec(per_batch, 0), _mod_spec(per_batch, 1)] + [_const_spec(w) for w in weights],
        out_specs=out_specs,
        out_shape=out_shape,
        scratch_shapes=scratch,
        compiler_params=pltpu.CompilerParams(dimension_semantics=("arbitrary", "arbitrary"),
                                             vmem_limit_bytes=VMEM_LIMIT),
        name=name,
    )(xin, mod, mod, *weights)


def _outproj_call(xin, ys, colmajor, mod, w_out, ln_g, ln_b, per_batch, tm, name):
    B, n, _ = xin.shape
    if any(colmajor):
        assert tm == TM_GRID and n == GRID_W * GRID_W

    def kern(x_ref, ya, yb, yc, yd, g_ref, w_ref, lg_ref, lb_ref, o_ref, *scr):
        acc = jnp.zeros((tm, D_MODEL), f32)
        for gi, (y_ref, cm) in enumerate(zip((ya, yb, yc, yd), colmajor)):
            if not cm:
                acc = acc + jnp.dot(y_ref[0].astype(bf16), w_ref[gi * G:(gi + 1) * G, :],
                                    preferred_element_type=f32)
                continue
            buf = scr[0]
            for c in range(GRID_W):
                for k in range(G // LANES):
                    buf[k, pl.ds(c, HALO, stride=GRID_W), :] = y_ref[0, :, c * G + LANES * k:c * G + LANES * (k + 1)]
            for k in range(G // LANES):
                acc = acc + jnp.dot(buf[k].astype(bf16), w_ref[gi * G + LANES * k:gi * G + LANES * (k + 1), :],
                                    preferred_element_type=f32)
        o_ref[0] = _layer_norm_rows(DEEPNORM_ALPHA * x_ref[0] + g_ref[0] * acc, lg_ref[...], lb_ref[...])

    yspecs = [pl.BlockSpec((1, HALO, GRID_W * G), lambda b, i: (b, i, 0)) if cm
              else pl.BlockSpec((1, tm, G), lambda b, i: (b, i, 0)) for cm in colmajor]
    scratch = [pltpu.VMEM((G // LANES, tm, LANES), f32)] if any(colmajor) else []
    return pl.pallas_call(
        kern,
        grid=(B, n // tm),
        in_specs=[pl.BlockSpec((1, tm, D_MODEL), lambda b, i: (b, i, 0))] + yspecs
                 + [_mod_spec(per_batch, 2), _const_spec(w_out), _const_spec(ln_g), _const_spec(ln_b)],
        out_specs=pl.BlockSpec((1, tm, D_MODEL), lambda b, i: (b, i, 0)),
        out_shape=jax.ShapeDtypeStruct((B, n, D_MODEL), f32),
        scratch_shapes=scratch,
        compiler_params=pltpu.CompilerParams(dimension_semantics=("arbitrary", "arbitrary"),
                                             vmem_limit_bytes=VMEM_LIMIT),
        name=name,
    )(xin, *ys, mod, w_out, ln_g, ln_b)


def _mlp_call(xin, mod, w1, w2, ln_g, ln_b, per_batch, tm, name):
    B, n, _ = xin.shape
    tf = 1024

    def kern(x_ref, sh_ref, sc_ref, g_ref, w1_ref, w2_ref, lg_ref, lb_ref, o_ref):
        xv = x_ref[0]
        hm = (xv * (1.0 + sc_ref[0]) + sh_ref[0]).astype(bf16)
        acc = jnp.zeros((tm, D_MODEL), f32)
        for j in range(D_FF // tf):
            a = jnp.maximum(jnp.dot(hm, w1_ref[:, j * tf:(j + 1) * tf], preferred_element_type=f32), 0.0)
            acc = acc + jnp.dot((a * a).astype(bf16), w2_ref[j * tf:(j + 1) * tf, :], preferred_element_type=f32)
        o_ref[0] = _layer_norm_rows(DEEPNORM_ALPHA * xv + g_ref[0] * acc, lg_ref[...], lb_ref[...])

    return pl.pallas_call(
        kern,
        grid=(B, n // tm),
        in_specs=[pl.BlockSpec((1, tm, D_MODEL), lambda b, i: (b, i, 0)),
                  _mod_spec(per_batch, 3), _mod_spec(per_batch, 4), _mod_spec(per_batch, 5),
                  _const_spec(w1), _const_spec(w2), _const_spec(ln_g), _const_spec(ln_b)],
        out_specs=pl.BlockSpec((1, tm, D_MODEL), lambda b, i: (b, i, 0)),
        out_shape=jax.ShapeDtypeStruct((B, n, D_MODEL), f32),
        compiler_params=pltpu.CompilerParams(dimension_semantics=("arbitrary", "arbitrary"),
                                             vmem_limit_bytes=VMEM_LIMIT),
        name=name,
    )(xin, mod, mod, mod, w1, w2, ln_g, ln_b)


def _pad_lanes(t, start, width=SCALAR_W):
    flat = t.reshape(1, -1).astype(f32)
    return jnp.pad(flat, ((0, 0), (start, width - start - flat.shape[1])))


def _split_w_in(w):
    oa, ob, oc = W_A, W_A + W_B, W_A + W_B + (3 * G + 4 * H + G)
    wa, wb, wc, wd = w[:, :oa], w[:, oa:ob], w[:, ob:oc], w[:, oc:]

    def relayout(t):
        qkv, scal, gate = t[:, :3 * G], t[:, 3 * G:3 * G + 4 * H], t[:, 3 * G + 4 * H:]
        return jnp.concatenate([qkv, gate, scal, jnp.zeros((t.shape[0], SCALAR_W - 4 * H), t.dtype)], axis=1)

    return [t.astype(bf16) for t in (wa, wb, relayout(wc), relayout(wd))]


def kernel(x, c, ctx, c_ctx, ada_w, ada_b, w_in, w_out, ln1_g, ln1_b, ln2_g, ln2_b, mlp_w1, mlp_w2,
           hgrn_gamma, hgrn_norm_g, rwkv_mu, rwkv_w0, rwkv_w2, rwkv_a0, rwkv_a2, rwkv_g2, rwkv_k_k,
           rwkv_k_a, rwkv_r_k, rwkv_ln_g, rwkv_ln_b, gdn_conv, gdn_a_log, gdn_dt_bias, gdn_norm_g,
           mlstm_i_bias, mlstm_f_bias, mlstm_norm_g):
    B = x.shape[0]
    depth = ada_w.shape[0]
    cc = jnp.concatenate([c, c_ctx[None, :], jnp.zeros((8 - B - 1, D_MODEL), f32)], axis=0)
    mods = _ada_call(cc, ada_w, ada_b)
    lb_cum = jnp.cumsum(jax.nn.softmax(hgrn_gamma.astype(f32), axis=0), axis=0)
    row = lambda t: t.reshape(1, -1).astype(f32)
    state = lambda k: pltpu.VMEM((B * NP, P2, k * P2), f32)
    ext = lambda w: pltpu.VMEM((B, C + 2 * HALO, w), f32)

    for layer in range(depth):
        last = layer == depth - 1
        odd = layer % 2 == 1
        mod_lat = mods[layer, :B].reshape(B, 1, 6 * D_MODEL)
        mod_ctx = mods[layer, B:B + 1].reshape(1, 1, 6 * D_MODEL)
        w_groups = _split_w_in(w_in[layer])
        cm_groups = [odd, odd, not odd, not odd]
        raster = [False] * len(cm_groups)
        p_lat = _proj_call(x, mod_lat, w_groups, cm_groups, True, TM_GRID, "proj_lat")
        p_ctx = _proj_call(ctx, mod_ctx, w_groups, raster, False, 256, "proj_ctx")

        lb = lb_cum[layer] - lb_cum[0]
        mixer = lambda body, cm, halo, g, params, scratch: dict(
            body=body, colmajor=cm, use_halo=halo, p_ctx=p_ctx[g], p_lat=p_lat[g], params=params, scratch=scratch)
        hgrn = mixer(_hgrn_body, odd, False, 0, [lb, row(hgrn_norm_g[layer])], [state(1)])
        rwkv = mixer(_rwkv_body, odd, True, 1,
                     [row(rwkv_mu[layer]), rwkv_w0[layer], rwkv_w2[layer], rwkv_a0[layer], rwkv_a2[layer],
                      rwkv_g2[layer], row(rwkv_k_k[layer]), row(rwkv_k_a[layer]), row(rwkv_r_k[layer]),
                      row(rwkv_ln_g[layer]), row(rwkv_ln_b[layer])],
                     [state(1), ext(W_B)])
        gdn = mixer(_gdn_body, not odd, True, 2,
                    [gdn_conv[layer], _pad_lanes(gdn_a_log[layer], 2 * H), _pad_lanes(gdn_dt_bias[layer], 2 * H),
                     row(gdn_norm_g[layer])],
                    [state(1), ext(3 * G)])
        mlstm = mixer(_mlstm_body, not odd, False, 3,
                      [_pad_lanes(mlstm_i_bias[layer], 0), _pad_lanes(mlstm_f_bias[layer], 2 * H),
                       row(mlstm_norm_g[layer])],
                      [state(2), pltpu.VMEM((8, SCALAR_W), f32)])
        ys = _run_mixers([hgrn, rwkv, gdn, mlstm], "mixers")
        wo = w_out[layer].astype(bf16)
        w1 = mlp_w1[layer].astype(bf16)
        w2 = mlp_w2[layer].astype(bf16)
        g1, b1, g2, b2 = row(ln1_g[layer]), row(ln1_b[layer]), row(ln2_g[layer]), row(ln2_b[layer])
        x = _outproj_call(x, [y[1] for y in ys], cm_groups, mod_lat, wo, g1, b1, True, TM_GRID, "outproj_lat")
        x = _mlp_call(x, mod_lat, w1, w2, g2, b2, True, 256, "mlp_lat")
        if not last:
            ctx = _outproj_call(ctx, [y[0] for y in ys], raster, mod_ctx, wo, g1, b1, False, 256, "outproj_ctx")
            ctx = _mlp_call(ctx, mod_ctx, w1, w2, g2, b2, False, 256, "mlp_ctx")
    return x
```

```python
import math

import jax
import jax.numpy as jnp
from jax import lax
from jax.experimental import pallas as pl
from jax.experimental.pallas import tpu as pltpu

f32 = jnp.float32
bf16 = jnp.bfloat16

D_MODEL = 1024
DEPTH = 4
GRID_W = 64
GROUP_W = 256
HEAD_DIM = 64
GROUP_HEADS = 4
D_FF = 4 * D_MODEL
CHUNK = 64
SUB = 16
HALO = 8
RWKV_DECAY_RANK = 64
RWKV_AAA_RANK = 64
RWKV_GATE_RANK = 128
GDN_CONV_W = 5
LN_EPS = 1e-5
RWKV_GN_EPS = 64e-5
NORM_EPS = 1e-6
DEEPNORM_ALPHA = (2.0 * DEPTH) ** 0.25
SCALAR_W = 128
W_A = 5 * GROUP_W
W_B = 3 * GROUP_W + 2 * RWKV_DECAY_RANK + 2 * RWKV_AAA_RANK + RWKV_GATE_RANK
W_CD = 4 * GROUP_W + SCALAR_W
VMEM_LIMIT = 56 * 1024 * 1024

NN = (((1,), (0,)), ((), ()))
NT = (((1,), (1,)), ((), ()))
TN = (((0,), (0,)), ((), ()))

G, H, HD, C = GROUP_W, GROUP_HEADS, HEAD_DIM, CHUNK
P2 = 2 * HD
NP = H // 2
assert HD == C and P2 == 128


def _mm(a, b, dims=NN):
    return lax.dot_general(a, b, dims, preferred_element_type=f32)


def _dot(a, b, dims=NN):
    return _mm(a.astype(bf16), b.astype(bf16), dims)


def _split(x):
    hi = x.astype(bf16)
    lo = (x - hi.astype(f32)).astype(bf16)
    return hi, lo


def _dot3s(a, b, dims=NN):
    return _mm(a[0], b[0], dims) + (_mm(a[0], b[1], dims) + _mm(a[1], b[0], dims))


def _dot3(a, b, dims=NN):
    return _dot3s(_split(a), _split(b), dims)


def _split_exact(x):
    h = x.astype(bf16)
    r = x - h.astype(f32)
    mid = r.astype(bf16)
    lo = (r - mid.astype(f32)).astype(bf16)
    return h, mid, lo


def _dot_exact_lhs(m, x):
    mb = m.astype(bf16)
    h, mid, lo = _split_exact(x)
    return _mm(mb, h) + (_mm(mb, mid) + _mm(mb, lo))


def _dot_exact_rhs(x, m):
    mb = m.astype(bf16)
    h, mid, lo = _split_exact(x)
    return _mm(h, mb) + (_mm(mid, mb) + _mm(lo, mb))


def _head_sum(t):
    r, c = _iota2(G)
    return _dot_exact_rhs(t, _same_block(r, c, HD))


def _expand_heads(t, lane0):
    r = lax.broadcasted_iota(jnp.int32, (SCALAR_W, G), 0)
    c = lax.broadcasted_iota(jnp.int32, (SCALAR_W, G), 1)
    return _dot_exact_rhs(t, r == lane0 + jnp.right_shift(c, int(math.log2(HD))))


def _softplus(x):
    return jnp.maximum(x, 0.0) + jnp.log(1.0 + jnp.exp(-jnp.abs(x)))


def _log_sigmoid(x):
    return -_softplus(-x)


def _sigmoid(x):
    return jax.nn.sigmoid(x)


def _iota2(n):
    return (lax.broadcasted_iota(jnp.int32, (n, n), 0), lax.broadcasted_iota(jnp.int32, (n, n), 1))


def _same_block(r, c, size):
    sh = int(math.log2(size))
    return jnp.right_shift(r, sh) == jnp.right_shift(c, sh)


def _before(d, strict, n=C, block=None):
    r, c = _iota2(n)
    if d == 0:
        m = (c < r) if strict else (c <= r)
    else:
        m = (c > r) if strict else (c >= r)
    if block is not None and block != n:
        m = jnp.logical_and(m, _same_block(r, c, block))
    return m


def _heads3(t):
    return t.reshape(t.shape[0], H, HD)


def _head_l2(t):
    return t * lax.rsqrt(_head_sum(t * t) + 1e-12)


def _head_rms(t, g):
    return t * lax.rsqrt(_head_sum(t * t) * (1.0 / HD) + NORM_EPS) * g


def _pk_iota(n):
    r = lax.broadcasted_iota(jnp.int32, (n, P2), 0)
    c = jnp.bitwise_and(lax.broadcasted_iota(jnp.int32, (n, P2), 1), HD - 1)
    return r, c


def _pk_before(d, strict):
    r, c = _pk_iota(C)
    if d == 0:
        return (c < r) if strict else (c <= r)
    return (c > r) if strict else (c >= r)


def _bd(t):
    left = lax.broadcasted_iota(jnp.int32, t.shape, 1) < HD
    return jnp.concatenate([jnp.where(left, t, 0.0), jnp.where(left, 0.0, t)], axis=0).astype(bf16)


def _bd_mask(t):
    r = jnp.right_shift(lax.broadcasted_iota(jnp.int32, t.shape, 0), int(math.log2(HD)))
    c = jnp.bitwise_and(jnp.right_shift(lax.broadcasted_iota(jnp.int32, t.shape, 1), int(math.log2(HD))), 1)
    return jnp.where(r == c, t, 0.0)


def _tri_inverse_many(Ls):
    r, c = _pk_iota(C)
    eye = (r == c).astype(f32)
    blk8 = _same_block(r, c, 8)
    ps = [-jnp.where(blk8, L, 0.0) for L in Ls]
    xs = [eye + p for p in ps]
    for _ in range(2):
        ps = [_mm(p.astype(bf16), _bd(p)) for p in ps]
        yield
        xs = [x + _mm(x.astype(bf16), _bd(p)) for x, p in zip(xs, ps)]
        yield
    for s in (8, 16, 32):
        ring = jnp.logical_and(_same_block(r, c, 2 * s), jnp.logical_not(_same_block(r, c, s)))
        ts = [_mm(x.astype(bf16), _bd(jnp.where(ring, L, 0.0))) for x, L in zip(xs, Ls)]
        yield
        xs = [x - _mm(t.astype(bf16), _bd(x)) for x, t in zip(xs, ts)]
        yield
    return xs


def _interleave(gens):
    results = [None] * len(gens)
    active = list(range(len(gens)))
    while active:
        for i in list(active):
            try:
                next(gens[i])
            except StopIteration as stop:
                results[i] = stop.value
                active.remove(i)
    return results


def _layer_norm_rows(t, g, b):
    mu = jnp.mean(t, -1, keepdims=True)
    tc = t - mu
    var = jnp.mean(tc * tc, -1, keepdims=True)
    return tc * lax.rsqrt(var + LN_EPS) * g + b


def _last_rows(t, d, nb, rows):
    return [t[rows * bb + (rows - 1 if d == 0 else 0):rows * bb + (rows if d == 0 else 1)] for bb in range(nb)]


def _instances(nb):
    return [(bb, p) for bb in range(nb) for p in range(NP)]


def _rs(bb):
    return slice(C * bb, C * bb + C)


def _ps(p):
    return slice(P2 * p, P2 * p + P2)


def _pk_rows(rows_t, lane0, bb, p):
    h = lane0 + 2 * p
    return jnp.concatenate([rows_t[h:h + 1, _rs(bb)], rows_t[h + 1:h + 2, _rs(bb)]], axis=1)


def _assemble(o_inst, nb):
    return jnp.concatenate([jnp.concatenate(o_inst[bb * NP:(bb + 1) * NP], axis=1) for bb in range(nb)], axis=0)


def _hgrn_body(d, x, prev8, next8, of, first, lb_ref, ng_ref, s_ref):
    del prev8, next8
    nb = x.shape[0]
    inst = _instances(nb)

    @pl.when(first)
    def _():
        s_ref[...] = jnp.zeros(s_ref.shape, f32)

    xf = x.reshape(nb * C, x.shape[2])
    q = xf[:, 0:G]
    v = xf[:, G:2 * G]
    f = xf[:, (2 + d) * G:(3 + d) * G]
    g = xf[:, 4 * G:5 * G]
    lb = lb_ref[d:d + 1, :]
    a = jnp.log(lb)
    bb_ = jnp.log1p(-lb) + _log_sigmoid(f)
    log_f = jnp.maximum(a, bb_) + jnp.log(1.0 + jnp.exp(-jnp.abs(a - bb_)))
    k = (1.0 - lb) * _sigmoid(-f)

    ns = nb * SUB
    m16 = _before(d, False, ns, SUB).astype(f32)
    lr, lc_ = _iota2(G)
    seg = _same_block(lr, lc_, HD).astype(bf16)
    ri = lax.broadcasted_iota(jnp.int32, (nb, SUB, G), 1)
    n_sub = C // SUB
    o_sub = [None] * n_sub
    for sbi in range(n_sub):
        sb = sbi if d == 0 else n_sub - 1 - sbi
        pick = lambda t: jnp.concatenate([t[C * b + SUB * sb:C * b + SUB * sb + SUB] for b in range(nb)], axis=0)
        lf_s, q_s, k_s, v_s = pick(log_f), pick(q), pick(k), pick(v)
        cum = _dot_exact_lhs(m16, lf_s)
        cum3, q3, k3, v3 = (t.reshape(nb, SUB, G) for t in (cum, q_s, k_s, v_s))
        tot3 = cum3[:, SUB - 1:SUB, :] if d == 0 else cum3[:, 0:1, :]
        qe3 = q3 * jnp.exp(cum3)
        ke3 = k3 * jnp.exp(tot3 - cum3)
        ts = []
        for j in range(SUB):
            e = jnp.exp(jnp.minimum(cum3 - cum3[:, j:j + 1, :], 0.0))
            valid = (ri >= j) if d == 0 else (ri <= j)
            ts.append(jnp.where(valid, q3 * e * k3[:, j:j + 1, :], 0.0).astype(bf16).reshape(ns, G))
        yield
        rsum = _mm(jnp.concatenate(ts, axis=0), seg)
        yield
        od = jnp.zeros((nb, SUB, G), f32)
        for j in range(SUB):
            od = od + rsum[j * ns:(j + 1) * ns].reshape(nb, SUB, G) * v3[:, j:j + 1, :]
        ss = [s_ref[b * NP + p] for b, p in inst]
        oi = [_dot(qe3[b][:, _ps(p)], s, NT) for (b, p), s in zip(inst, ss)]
        for (b, p), s in zip(inst, ss):
            s_ref[b * NP + p] = (s * jnp.exp(tot3[b][:, _ps(p)])
                                 + _bd_mask(_dot(v3[b][:, _ps(p)], ke3[b][:, _ps(p)], TN)))
        o_sub[sb] = [jnp.concatenate(oi[b * NP:(b + 1) * NP], axis=1) + od[b] for b in range(nb)]
        yield
    o = jnp.concatenate([o_sub[sb][b] for b in range(nb) for sb in range(n_sub)], axis=0)
    if d == 0:
        return o
    return _head_rms(o + of, ng_ref[...]) * (g * _sigmoid(g))


def _gdn_body(d, x, prev8, next8, of, first, cw_ref, alog_ref, dtb_ref, ng_ref, s_ref, ext_ref):
    nb = x.shape[0]
    inst = _instances(nb)

    @pl.when(first)
    def _():
        s_ref[...] = jnp.zeros(s_ref.shape, f32)

    wq = 3 * G
    accs = []
    for b in range(nb):
        ext_ref[b, 0:HALO, :] = prev8[b][:, :wq]
        ext_ref[b, HALO:HALO + C, :] = x[b][:, :wq]
        ext_ref[b, HALO + C:2 * HALO + C, :] = next8[b][:, :wq]
        acc = jnp.zeros((C, wq), f32)
        for kk in range(GDN_CONV_W):
            acc = acc + cw_ref[kk:kk + 1, :] * ext_ref[b, pl.ds(HALO - GDN_CONV_W // 2 + kk, C), :]
        accs.append(acc)
    acc = jnp.concatenate(accs, axis=0)
    xf = x.reshape(nb * C, x.shape[2])
    qkv = acc * _sigmoid(acc)
    q = _head_l2(qkv[:, 0:G]) * HD ** -0.5
    k = _head_l2(qkv[:, G:2 * G])
    v = qkv[:, 2 * G:3 * G]
    g = xf[:, 3 * G:4 * G]
    sc = xf[:, 4 * G:4 * G + SCALAR_W]
    beta_all = _sigmoid(sc)
    la_all = -jnp.exp(alog_ref[...]) * _softplus(sc + dtb_ref[...])
    incl = _pk_before(d, False)
    strict = _pk_before(d, True)
    cum_all = _dot_exact_lhs(_before(d, False, nb * C, C).astype(f32), la_all)
    cum_rows = cum_all.T
    last_bc = jnp.concatenate([jnp.broadcast_to(t, (C, SCALAR_W)) for t in _last_rows(cum_all, d, nb, C)], axis=0)
    lb_, la_ = 4 * d, 8 + 4 * d
    nr = nb * C
    beta_e = _expand_heads(beta_all, lb_)
    cum_ld = _expand_heads(jnp.concatenate([cum_all, last_bc - cum_all], axis=0), la_)
    cum_e = cum_ld[:nr]
    ec_e = jnp.exp(cum_e)
    kb_f = k * beta_e
    vb_f = v * beta_e
    kbe_f = kb_f * ec_e
    qe_f = q * ec_e
    kl_f = k * jnp.exp(cum_ld[nr:])
    sl = lambda t: [t[_rs(b), _ps(p)] for b, p in inst]
    qh, kh, kb, vb, kbe, qe, kl, cc = sl(q), sl(k), sl(kb_f), sl(vb_f), sl(kbe_f), sl(qe_f), sl(kl_f), sl(cum_e)
    cr = [_pk_rows(cum_rows, la_, b, p) for b, p in inst]
    row_last = C - 1 if d == 0 else 0
    e_last = [ec_e[C * b + row_last:C * b + row_last + 1, _ps(p)] for b, p in inst]
    yield
    gam = [jnp.where(incl, jnp.exp(jnp.minimum(c_ - r_, 0.0)), 0.0) for c_, r_ in zip(cc, cr)]
    kk_qk = [_mm(jnp.concatenate([kb_, q_], axis=0).astype(bf16), _bd(k_), NT) * jnp.concatenate([g_, g_], axis=0)
             for kb_, q_, k_, g_ in zip(kb, qh, kh, gam)]
    yield
    Ts = yield from _tri_inverse_many([jnp.where(strict, a_[:C], 0.0) for a_ in kk_qk])
    sol = [_mm(T.astype(bf16), jnp.concatenate([_bd(vb_), _bd(kbe_)], axis=1)) for T, vb_, kbe_ in zip(Ts, vb, kbe)]
    yield
    ss = [s_ref[b * NP + p] for b, p in inst]
    ws_qs = [_dot(jnp.concatenate([so[:, P2:], qe_], axis=0), s) for so, qe_, s in zip(sol, qe, ss)]
    yield
    vn = [so[:, :P2] - t_[:C] for so, t_ in zip(sol, ws_qs)]
    o_inst = [t_[C:] + _mm(a_[C:].astype(bf16), _bd(vn_)) for t_, a_, vn_ in zip(ws_qs, kk_qk, vn)]
    yield
    for (b, p), s, el_, kl_, vn_ in zip(inst, ss, e_last, kl, vn):
        s_ref[b * NP + p] = el_ * s + _bd_mask(_dot(kl_, vn_, TN))
    o = _assemble(o_inst, nb)
    if d == 0:
        return o
    return _head_rms(o + of, ng_ref[...]) * (g * _sigmoid(g))


def _mlstm_body(d, x, prev8, next8, of, first, ib_ref, fb_ref, ng_ref, c_ref, m_ref):
    del prev8, next8
    nb = x.shape[0]
    inst = _instances(nb)

    @pl.when(first)
    def _():
        c_ref[...] = jnp.zeros(c_ref.shape, f32)
        m_ref[...] = jnp.zeros(m_ref.shape, f32)

    xf = x.reshape(nb * C, x.shape[2])
    q = xf[:, 0:G]
    k = xf[:, G:2 * G] * HD ** -0.5
    v = xf[:, 2 * G:3 * G]
    og = xf[:, 3 * G:4 * G]
    sc = xf[:, 4 * G:4 * G + SCALAR_W]
    li_all = sc + ib_ref[...]
    lf_all = _log_sigmoid(sc + fb_ref[...])
    incl = _pk_before(d, False)
    nr = nb * C
    stack = lambda rows: jnp.concatenate([jnp.broadcast_to(t, (C, SCALAR_W)) for t in rows], axis=0)
    b_all = _dot_exact_lhs(_before(d, False, nr, C).astype(f32), lf_all)
    li_sh = pltpu.roll(li_all, 2 * H, axis=1)
    m_rows = [m_ref[b:b + 1, :] for b in range(nb)]
    m_all = stack(m_rows)
    a_all = li_sh - b_all
    loc = jnp.bitwise_and(lax.broadcasted_iota(jnp.int32, (nr, SCALAR_W), 0), C - 1)
    run = a_all
    sh = 1
    while sh < C:
        if d == 0:
            shifted = jnp.where(loc >= sh, pltpu.roll(run, sh, axis=0), -jnp.inf)
        else:
            shifted = jnp.where(loc < C - sh, pltpu.roll(run, nr - sh, axis=0), -jnp.inf)
        run = jnp.maximum(run, shifted)
        sh *= 2
    mx_all = jnp.maximum(run, m_all)
    iw_all = jnp.exp(m_all - mx_all)
    eneg_all = jnp.exp(-(b_all + mx_all))
    bl_rows = _last_rows(b_all, d, nb, C)
    src_all = stack(bl_rows) - b_all + li_sh
    mnew_rows = [jnp.maximum(bl_ + m_, jnp.max(src_all[_rs(b)], 0, keepdims=True))
                 for b, (bl_, m_) in enumerate(zip(bl_rows, m_rows))]
    cw_rows = [jnp.exp(bl_ + m_ - mn_) for bl_, m_, mn_ in zip(bl_rows, m_rows, mnew_rows)]
    wj_all = jnp.exp(src_all - stack(mnew_rows))
    a_rows = a_all.T
    ln = 2 * H + 4 * d
    ex = _expand_heads(jnp.concatenate([mx_all, iw_all, eneg_all, wj_all, stack(cw_rows)], axis=0), ln)
    mx_e, iw_e, en_e, wj_e, cw_e = (ex[i * nr:(i + 1) * nr] for i in range(5))
    sl = lambda t: [t[_rs(b), _ps(p)] for b, p in inst]
    ones = jnp.ones((C, P2), f32)
    bd_ones = _bd(ones)
    qh, qi, kw, mxc, enc = sl(q), sl(q * iw_e), sl(k * wj_e), sl(mx_e), sl(en_e)
    kh, vh = sl(k), sl(v)
    arow = [_pk_rows(a_rows, ln, b, p) for b, p in inst]
    cw = [cw_e[C * b:C * b + 1, _ps(p)] for b, p in inst]
    yield
    dw = [jnp.where(incl, jnp.exp(jnp.minimum(ar_ - mc_, 0.0)), 0.0) for ar_, mc_ in zip(arow, mxc)]
    s_ = [_mm(q_.astype(bf16), _bd(k_), NT) * dw_ for q_, k_, dw_ in zip(qh, kh, dw)]
    yield
    cst = [c_ref[b * NP + p] for b, p in inst]
    num = [_dot(qi_, c_) + _mm(s__.astype(bf16), jnp.concatenate([_bd(v_), bd_ones], axis=1))
           for qi_, c_, s__, v_ in zip(qi, cst, s_, vh)]
    yield
    o_inst = [n_[:, :P2] * (1.0 / jnp.maximum(jnp.abs(n_[:, P2:]), en_)) for n_, en_ in zip(num, enc)]
    for (b, p), c_, cw_, kw_, v_ in zip(inst, cst, cw, kw, vh):
        c_ref[b * NP + p] = (jnp.concatenate([cw_, cw_], axis=1) * c_
                             + _bd_mask(_dot(kw_, jnp.concatenate([v_, ones], axis=1), TN)))
    for b in range(nb):
        m_ref[b:b + 1, :] = mnew_rows[b]
    o = _assemble(o_inst, nb)
    if d == 0:
        return o
    return _head_rms(o + of, ng_ref[...]) * _sigmoid(og)


def _rwkv_body(d, x, prev8, next8, of, first, mu_ref, w0_ref, w2_ref, a0_ref, a2_ref, g2_ref,
               kk_ref, ka_ref, rk_ref, lng_ref, lnb_ref, s_ref, ext_ref):
    nb = x.shape[0]
    inst = _instances(nb)

    @pl.when(first)
    def _():
        s_ref[...] = jnp.zeros(s_ref.shape, f32)

    nbs = []
    for b in range(nb):
        ext_ref[b, 0:HALO, :] = prev8[b]
        ext_ref[b, HALO:HALO + C, :] = x[b]
        ext_ref[b, HALO + C:2 * HALO + C, :] = next8[b]
        nbs.append(ext_ref[b, pl.ds(HALO - 1, C), :] + ext_ref[b, pl.ds(HALO + 1, C), :])
    xf = x.reshape(nb * C, x.shape[2])
    xs = xf + mu_ref[...] * (0.5 * jnp.concatenate(nbs, axis=0) - xf)
    r = xs[:, 0:G]
    k = xs[:, G:2 * G]
    v = xs[:, 2 * G:3 * G]
    o_w = 3 * G
    o_a = o_w + 2 * RWKV_DECAY_RANK
    o_g = o_a + 2 * RWKV_AAA_RANK

    def a_of(dd):
        ad = xs[:, o_a + RWKV_AAA_RANK * dd:o_a + RWKV_AAA_RANK * (dd + 1)]
        return _sigmoid(a0_ref[dd:dd + 1, :] + _dot3(ad, a2_ref[dd]))

    wd = xs[:, o_w + RWKV_DECAY_RANK * d:o_w + RWKV_DECAY_RANK * (d + 1)]
    w = -_softplus(-(w0_ref[d:d + 1, :] + _dot3(jnp.tanh(wd), w2_ref[d]))) - 0.5
    logw = -jnp.exp(w)
    a = a_of(d)
    kk = _head_l2(k * kk_ref[...])
    kd = k * (1.0 + (a - 1.0) * ka_ref[...])
    beta = kk * a
    incl = _pk_before(d, False)
    strict = _pk_before(d, True)
    lc = _dot_exact_lhs(_before(d, False, nb * C, C).astype(f32), logw)
    tot_all = _last_rows(lc, d, nb, C)
    tot = jnp.concatenate([jnp.broadcast_to(t, (C, G)) for t in tot_all], axis=0)
    e_neg = jnp.exp(-lc)
    e_tot = jnp.exp(tot - lc)
    kp = kk * jnp.exp(lc - logw)
    rp = r * jnp.exp(lc)
    bn = beta * e_neg
    kn = kd * e_neg
    ke = kd * e_tot
    be = beta * e_tot
    sl = lambda t: [t[_rs(b), _ps(p)] for b, p in inst]
    kp_, rp_, bn_, kn_, ke_, be_, vh = sl(kp), sl(rp), sl(bn), sl(kn), sl(ke), sl(be), sl(v)
    ss = [s_ref[b * NP + p] for b, p in inst]
    lhs = [jnp.concatenate([p_, t], axis=0).astype(bf16) for p_, t in zip(kp_, rp_)]
    yield
    quad = [_mm(l_, jnp.concatenate([_bd(b_), _bd(k_)], axis=0), NT) for l_, b_, k_ in zip(lhs, bn_, kn_)]
    ps_rs = [_mm(l_, s.astype(bf16), NT) for l_, s in zip(lhs, ss)]
    yield
    Ts = yield from _tri_inverse_many([jnp.where(strict, q_[:C, :P2], 0.0) for q_ in quad])
    rhs = [t_[:C] + _mm(jnp.where(strict, q_[:C, P2:], 0.0).astype(bf16), _bd(v_))
           for t_, q_, v_ in zip(ps_rs, quad, vh)]
    yield
    u = [_mm(T.astype(bf16), _bd(r_)) for T, r_ in zip(Ts, rhs)]
    yield
    o_inst = [t_[C:] + _mm(jnp.concatenate([jnp.where(incl, q_[C:, P2:], 0.0), jnp.where(incl, -q_[C:, :P2], 0.0)],
                                           axis=1).astype(bf16),
                           jnp.concatenate([_bd(v_), _bd(u_)], axis=0))
              for t_, q_, v_, u_ in zip(ps_rs, quad, vh, u)]
    yield
    for (b, p), s, v_, ke__, u_, be__ in zip(inst, ss, vh, ke_, u, be_):
        s_ref[b * NP + p] = (s * jnp.exp(tot_all[b][:, _ps(p)])
                             + _bd_mask(_dot(jnp.concatenate([v_, -u_], axis=0),
                                             jnp.concatenate([ke__, be__], axis=0), TN)))
    o = _assemble(o_inst, nb)
    if d == 0:
        return o
    kd_other = k * (1.0 + (a_of(1 - d) - 1.0) * ka_ref[...])
    bonus = _head_sum(r * (kd + kd_other) * rk_ref[...]) * v
    t = o + of
    tc = t - _head_sum(t) * (1.0 / HD)
    var = _head_sum(tc * tc) * (1.0 / HD)
    gn = tc * lax.rsqrt(var + RWKV_GN_EPS) * lng_ref[...] + lnb_ref[...]
    gate = _dot(_sigmoid(xs[:, o_g:o_g + RWKV_GATE_RANK]), g2_ref[...])
    return (gn + bonus) * gate


def _mixer_io(m, d, of_ctx, of_lat):
    p_ctx, p_lat, colmajor, use_halo, params = m["p_ctx"], m["p_lat"], m["colmajor"], m["use_halo"], m["params"]
    B, n_ctx, W = p_ctx.shape
    if colmajor:
        assert p_lat.shape == (B, C, GRID_W * W)
        n_lat = GRID_W * C
    else:
        n_lat = p_lat.shape[1]
    nc, nl = n_ctx // C, n_lat // C
    have_of = d == 1

    def cj(s):
        return jnp.clip(s if d == 0 else nc - 1 - s, 0, nc - 1)

    def lj(s):
        return jnp.clip((s - nc) if d == 0 else nl - 1 - (s - nc), 0, nl - 1)

    def lat_view(t, w):
        return t

    def lat_spec(w):
        if colmajor:
            return pl.BlockSpec((B, C, w), lambda s: (0, 0, lj(s)))
        return pl.BlockSpec((B, C, w), lambda s: (0, lj(s), 0))

    def ctx_spec(w):
        return pl.BlockSpec((B, C, w), lambda s: (0, cj(s), 0))

    rpc = C // HALO
    operands = [p_ctx, lat_view(p_lat, W)]
    in_specs = [ctx_spec(W), lat_spec(W)]
    if use_halo:
        operands += [p_ctx, p_ctx, lat_view(p_lat, W), lat_view(p_lat, W)]
        in_specs += [
            pl.BlockSpec((B, HALO, W), lambda s: (0, jnp.maximum(rpc * cj(s) - 1, 0), 0)),
            pl.BlockSpec((B, HALO, W), lambda s: (0, jnp.minimum(rpc * (cj(s) + 1), rpc * nc - 1), 0)),
        ]
        if colmajor:
            in_specs += [
                pl.BlockSpec((B, HALO, W), lambda s: (0, rpc - 1, jnp.maximum(lj(s) - 1, 0))),
                pl.BlockSpec((B, HALO, W), lambda s: (0, 0, jnp.minimum(lj(s) + 1, nl - 1))),
            ]
        else:
            in_specs += [
                pl.BlockSpec((B, HALO, W), lambda s: (0, jnp.maximum(rpc * lj(s) - 1, 0), 0)),
                pl.BlockSpec((B, HALO, W), lambda s: (0, jnp.minimum(rpc * (lj(s) + 1), rpc * nl - 1), 0)),
            ]
    if have_of:
        operands += [of_ctx, lat_view(of_lat, G)]
        in_specs += [ctx_spec(G), lat_spec(G)]
    n_data = len(operands)
    for p in params:
        operands.append(p)
        in_specs.append(pl.BlockSpec(p.shape, lambda s, _n=p.ndim: (0,) * _n))

    def start(refs, scr, s):
        data, par = refs[:n_data], refs[n_data:]
        is_ctx = s < nc
        pick = lambda a, b: jnp.where(is_ctx, a[...], b[...])
        x = pick(data[0], data[1])
        i = 2
        prev8 = next8 = None
        if use_halo:
            j = jnp.where(is_ctx, cj(s), lj(s))
            jmax = jnp.where(is_ctx, nc - 1, nl - 1)
            prev8 = jnp.where(j > 0, pick(data[2], data[4]), 0.0)
            next8 = jnp.where(j < jmax, pick(data[3], data[5]), 0.0)
            i = 6
        of = pick(data[i], data[i + 1]).reshape(B * C, G) if have_of else None
        return m["body"](d, x, prev8, next8, of, s == 0, *par, *scr)

    return dict(operands=operands, in_specs=in_specs, out_specs=[ctx_spec(G), lat_spec(G)],
                out_shape=[jax.ShapeDtypeStruct((B, n_ctx, G), f32),
                           jax.ShapeDtypeStruct((B, C, GRID_W * G) if colmajor else (B, n_lat, G), f32)],
                start=start, steps=nc + nl, nc=nc)


def _mixer_group_call(mixers, d, ofs, name):
    ios = [_mixer_io(m, d, *of) for m, of in zip(mixers, ofs)]
    steps, nc = ios[0]["steps"], ios[0]["nc"]
    n_in = [len(io["operands"]) for io in ios]
    n_scr = [len(m["scratch"]) for m in mixers]

    def kern(*refs):
        ins, pos = [], 0
        for n in n_in:
            ins.append(refs[pos:pos + n])
            pos += n
        outs = [refs[pos + 2 * i:pos + 2 * i + 2] for i in range(len(ios))]
        pos += 2 * len(ios)
        scrs = []
        for n in n_scr:
            scrs.append(refs[pos:pos + n])
            pos += n
        s = pl.program_id(0)
        is_ctx = s < nc
        ys = _interleave([io["start"](i_, scr, s) for io, i_, scr in zip(ios, ins, scrs)])
        for (oc_ref, ol_ref), y in zip(outs, ys):
            y3 = y.reshape(oc_ref.shape)

            @pl.when(is_ctx)
            def _():
                oc_ref[...] = y3

            @pl.when(jnp.logical_not(is_ctx))
            def _():
                ol_ref[...] = y3

    res = pl.pallas_call(
        kern,
        grid=(steps,),
        in_specs=[sp for io in ios for sp in io["in_specs"]],
        out_specs=[sp for io in ios for sp in io["out_specs"]],
        out_shape=[sh for io in ios for sh in io["out_shape"]],
        scratch_shapes=[sc for m in mixers for sc in m["scratch"]],
        compiler_params=pltpu.CompilerParams(dimension_semantics=("arbitrary",),
                                             vmem_limit_bytes=VMEM_LIMIT),
        name=name,
    )(*[op for io in ios for op in io["operands"]])
    return [(res[2 * i], res[2 * i + 1]) for i in range(len(ios))]


def _run_mixers(mixers, name):
    fwd = _mixer_group_call(mixers, 0, [(None, None)] * len(mixers), name + "_fwd")
    return _mixer_group_call(mixers, 1, fwd, name + "_bwd")


def _run_mixer(body, colmajor, use_halo, p_ctx, p_lat, params, scratch, name):
    m = dict(body=body, colmajor=colmajor, use_halo=use_halo, p_ctx=p_ctx, p_lat=p_lat, params=params, scratch=scratch)
    return _run_mixers([m], name)[0]


def _ada_call(cc, ada_w, ada_b):
    L, _, n6 = ada_w.shape
    tn = 1536

    def kern(c_ref, w_ref, b_ref, o_ref):
        cv = c_ref[...]
        o_ref[0] = _dot3(cv * _sigmoid(cv), w_ref[0]) + b_ref[0]

    return pl.pallas_call(
        kern,
        grid=(L, n6 // tn),
        in_specs=[pl.BlockSpec(cc.shape, lambda l, j: (0, 0)),
                  pl.BlockSpec((1, D_MODEL, tn), lambda l, j: (l, 0, j)),
                  pl.BlockSpec((1, 1, tn), lambda l, j: (l, 0, j))],
        out_specs=pl.BlockSpec((1, cc.shape[0], tn), lambda l, j: (l, 0, j)),
        out_shape=jax.ShapeDtypeStruct((L, cc.shape[0], n6), f32),
        compiler_params=pltpu.CompilerParams(dimension_semantics=("arbitrary", "arbitrary"),
                                             vmem_limit_bytes=VMEM_LIMIT),
        name="ada",
    )(cc, ada_w, ada_b.reshape(L, 1, n6))


def _mod_spec(per_batch, k):
    if per_batch:
        return pl.BlockSpec((1, 1, D_MODEL), lambda b, i: (b, 0, k))
    return pl.BlockSpec((1, 1, D_MODEL), lambda b, i: (0, 0, k))


def _const_spec(a):
    return pl.BlockSpec(a.shape, lambda b, i, _n=a.ndim: (0,) * _n)


TM_GRID = HALO * GRID_W


def _tile_perm(tm, to_cols):
    dst = lax.broadcasted_iota(jnp.int32, (tm, tm), 0)
    src = lax.broadcasted_iota(jnp.int32, (tm, tm), 1)
    if to_cols:
        want = jnp.bitwise_and(dst, HALO - 1) * GRID_W + jnp.right_shift(dst, int(math.log2(HALO)))
    else:
        want = jnp.bitwise_and(dst, GRID_W - 1) * HALO + jnp.right_shift(dst, int(math.log2(GRID_W)))
    return (src == want).astype(bf16)


def _proj_call(xin, mod, weights, colmajor, per_batch, tm, name):
    B, n, _ = xin.shape
    widths = [w.shape[1] for w in weights]
    nw = len(weights)
    if any(colmajor):
        assert tm == TM_GRID and n == GRID_W * GRID_W

    def kern(x_ref, sh_ref, sc_ref, *rest):
        w_refs, o_refs = rest[:nw], rest[nw:]
        hm = (x_ref[0] * (1.0 + sc_ref[0]) + sh_ref[0]).astype(bf16)
        if any(colmajor):
            hm_cols = jnp.dot(_tile_perm(tm, True), hm, preferred_element_type=f32).astype(bf16)
        for w_ref, o_ref, cm, wd in zip(w_refs, o_refs, colmajor, widths):
            if not cm:
                o_ref[0] = jnp.dot(hm, w_ref[...], preferred_element_type=f32)
                continue
            res = jnp.dot(hm_cols, w_ref[...], preferred_element_type=f32)
            for c in range(GRID_W):
                o_ref[0, :, c * wd:(c + 1) * wd] = res[c * HALO:(c + 1) * HALO, :]

    out_specs = [pl.BlockSpec((1, HALO, GRID_W * w), lambda b, i: (b, i, 0)) if cm
                 else pl.BlockSpec((1, tm, w), lambda b, i: (b, i, 0)) for w, cm in zip(widths, colmajor)]
    out_shape = [jax.ShapeDtypeStruct((B, GRID_W, GRID_W * w) if cm else (B, n, w), f32)
                 for w, cm in zip(widths, colmajor)]
    return pl.pallas_call(
        kern,
        grid=(B, n // tm),
        in_specs=[pl.BlockSpec((1, tm, D_MODEL), lambda b, i: (b, i, 0)),
                  _mod_spec(per_batch, 0), _mod_spec(per_batch, 1)] + [_const_spec(w) for w in weights],
        out_specs=out_specs,
        out_shape=out_shape,
        compiler_params=pltpu.CompilerParams(dimension_semantics=("arbitrary", "arbitrary"),
                                             vmem_limit_bytes=VMEM_LIMIT),
        name=name,
    )(xin, mod, mod, *weights)


def _outproj_call(xin, ys, colmajor, mod, w_out, ln_g, ln_b, per_batch, tm, name):
    B, n, _ = xin.shape
    if any(colmajor):
        assert tm == TM_GRID and n == GRID_W * GRID_W

    def kern(x_ref, ya, yb, yc, yd, g_ref, w_ref, lg_ref, lb_ref, o_ref):
        acc = jnp.zeros((tm, D_MODEL), f32)
        for gi, (y_ref, cm) in enumerate(zip((ya, yb, yc, yd), colmajor)):
            if cm:
                y_cols = jnp.concatenate([y_ref[0, :, c * G:(c + 1) * G] for c in range(GRID_W)], axis=0)
                yv = jnp.dot(_tile_perm(tm, False), y_cols.astype(bf16), preferred_element_type=f32).astype(bf16)
            else:
                yv = y_ref[0].astype(bf16)
            acc = acc + jnp.dot(yv, w_ref[gi * G:(gi + 1) * G, :], preferred_element_type=f32)
        o_ref[0] = _layer_norm_rows(DEEPNORM_ALPHA * x_ref[0] + g_ref[0] * acc, lg_ref[...], lb_ref[...])

    yspecs = [pl.BlockSpec((1, HALO, GRID_W * G), lambda b, i: (b, i, 0)) if cm
              else pl.BlockSpec((1, tm, G), lambda b, i: (b, i, 0)) for cm in colmajor]
    return pl.pallas_call(
        kern,
        grid=(B, n // tm),
        in_specs=[pl.BlockSpec((1, tm, D_MODEL), lambda b, i: (b, i, 0))] + yspecs
                 + [_mod_spec(per_batch, 2), _const_spec(w_out), _const_spec(ln_g), _const_spec(ln_b)],
        out_specs=pl.BlockSpec((1, tm, D_MODEL), lambda b, i: (b, i, 0)),
        out_shape=jax.ShapeDtypeStruct((B, n, D_MODEL), f32),
        compiler_params=pltpu.CompilerParams(dimension_semantics=("arbitrary", "arbitrary"),
                                             vmem_limit_bytes=VMEM_LIMIT),
        name=name,
    )(xin, *ys, mod, w_out, ln_g, ln_b)


def _mlp_call(xin, mod, w1, w2, ln_g, ln_b, per_batch, tm, name):
    B, n, _ = xin.shape
    tf = 1024

    def kern(x_ref, sh_ref, sc_ref, g_ref, w1_ref, w2_ref, lg_ref, lb_ref, o_ref):
        xv = x_ref[0]
        hm = (xv * (1.0 + sc_ref[0]) + sh_ref[0]).astype(bf16)
        acc = jnp.zeros((tm, D_MODEL), f32)
        for j in range(D_FF // tf):
            a = jnp.maximum(jnp.dot(hm, w1_ref[:, j * tf:(j + 1) * tf], preferred_element_type=f32), 0.0)
            acc = acc + jnp.dot((a * a).astype(bf16), w2_ref[j * tf:(j + 1) * tf, :], preferred_element_type=f32)
        o_ref[0] = _layer_norm_rows(DEEPNORM_ALPHA * xv + g_ref[0] * acc, lg_ref[...], lb_ref[...])

    return pl.pallas_call(
        kern,
        grid=(B, n // tm),
        in_specs=[pl.BlockSpec((1, tm, D_MODEL), lambda b, i: (b, i, 0)),
                  _mod_spec(per_batch, 3), _mod_spec(per_batch, 4), _mod_spec(per_batch, 5),
                  _const_spec(w1), _const_spec(w2), _const_spec(ln_g), _const_spec(ln_b)],
        out_specs=pl.BlockSpec((1, tm, D_MODEL), lambda b, i: (b, i, 0)),
        out_shape=jax.ShapeDtypeStruct((B, n, D_MODEL), f32),
        compiler_params=pltpu.CompilerParams(dimension_semantics=("arbitrary", "arbitrary"),
                                             vmem_limit_bytes=VMEM_LIMIT),
        name=name,
    )(xin, mod, mod, mod, w1, w2, ln_g, ln_b)


def _pad_lanes(t, start, width=SCALAR_W):
    flat = t.reshape(1, -1).astype(f32)
    return jnp.pad(flat, ((0, 0), (start, width - start - flat.shape[1])))


def _split_w_in(w):
    oa, ob, oc = W_A, W_A + W_B, W_A + W_B + (3 * G + 4 * H + G)
    wa, wb, wc, wd = w[:, :oa], w[:, oa:ob], w[:, ob:oc], w[:, oc:]

    def relayout(t):
        qkv, scal, gate = t[:, :3 * G], t[:, 3 * G:3 * G + 4 * H], t[:, 3 * G + 4 * H:]
        return jnp.concatenate([qkv, gate, scal, jnp.zeros((t.shape[0], SCALAR_W - 4 * H), t.dtype)], axis=1)

    return [t.astype(bf16) for t in (wa, wb, relayout(wc), relayout(wd))]


def kernel(x, c, ctx, c_ctx, ada_w, ada_b, w_in, w_out, ln1_g, ln1_b, ln2_g, ln2_b, mlp_w1, mlp_w2,
           hgrn_gamma, hgrn_norm_g, rwkv_mu, rwkv_w0, rwkv_w2, rwkv_a0, rwkv_a2, rwkv_g2, rwkv_k_k,
           rwkv_k_a, rwkv_r_k, rwkv_ln_g, rwkv_ln_b, gdn_conv, gdn_a_log, gdn_dt_bias, gdn_norm_g,
           mlstm_i_bias, mlstm_f_bias, mlstm_norm_g):
    B = x.shape[0]
    depth = ada_w.shape[0]
    cc = jnp.concatenate([c, c_ctx[None, :], jnp.zeros((8 - B - 1, D_MODEL), f32)], axis=0)
    mods = _ada_call(cc, ada_w, ada_b)
    lb_cum = jnp.cumsum(jax.nn.softmax(hgrn_gamma.astype(f32), axis=0), axis=0)
    row = lambda t: t.reshape(1, -1).astype(f32)
    state = lambda k: pltpu.VMEM((B * NP, P2, k * P2), f32)
    ext = lambda w: pltpu.VMEM((B, C + 2 * HALO, w), f32)

    for layer in range(depth):
        last = layer == depth - 1
        odd = layer % 2 == 1
        mod_lat = mods[layer, :B].reshape(B, 1, 6 * D_MODEL)
        mod_ctx = mods[layer, B:B + 1].reshape(1, 1, 6 * D_MODEL)
        w_groups = _split_w_in(w_in[layer])
        cm_groups = [odd, odd, not odd, not odd]
        raster = [False] * len(cm_groups)
        p_lat = _proj_call(x, mod_lat, w_groups, cm_groups, True, TM_GRID, "proj_lat")
        p_ctx = _proj_call(ctx, mod_ctx, w_groups, raster, False, 256, "proj_ctx")

        lb = lb_cum[layer] - lb_cum[0]
        mixer = lambda body, cm, halo, g, params, scratch: dict(
            body=body, colmajor=cm, use_halo=halo, p_ctx=p_ctx[g], p_lat=p_lat[g], params=params, scratch=scratch)
        hgrn = mixer(_hgrn_body, odd, False, 0, [lb, row(hgrn_norm_g[layer])], [state(1)])
        rwkv = mixer(_rwkv_body, odd, True, 1,
                     [row(rwkv_mu[layer]), rwkv_w0[layer], rwkv_w2[layer], rwkv_a0[layer], rwkv_a2[layer],
                      rwkv_g2[layer], row(rwkv_k_k[layer]), row(rwkv_k_a[layer]), row(rwkv_r_k[layer]),
                      row(rwkv_ln_g[layer]), row(rwkv_ln_b[layer])],
                     [state(1), ext(W_B)])
        gdn = mixer(_gdn_body, not odd, True, 2,
                    [gdn_conv[layer], _pad_lanes(gdn_a_log[layer], 2 * H), _pad_lanes(gdn_dt_bias[layer], 2 * H),
                     row(gdn_norm_g[layer])],
                    [state(1), ext(3 * G)])
        mlstm = mixer(_mlstm_body, not odd, False, 3,
                      [_pad_lanes(mlstm_i_bias[layer], 0), _pad_lanes(mlstm_f_bias[layer], 2 * H),
                       row(mlstm_norm_g[layer])],
                      [state(2), pltpu.VMEM((8, SCALAR_W), f32)])
        ys = _run_mixers([hgrn, rwkv, gdn, mlstm], "mixers")
        wo = w_out[layer].astype(bf16)
        w1 = mlp_w1[layer].astype(bf16)
        w2 = mlp_w2[layer].astype(bf16)
        g1, b1, g2, b2 = row(ln1_g[layer]), row(ln1_b[layer]), row(ln2_g[layer]), row(ln2_b[layer])
        x = _outproj_call(x, [y[1] for y in ys], cm_groups, mod_lat, wo, g1, b1, True, TM_GRID, "outproj_lat")
        x = _mlp_call(x, mod_lat, w1, w2, g2, b2, True, 256, "mlp_lat")
        if not last:
            ctx = _outproj_call(ctx, [y[0] for y in ys], raster, mod_ctx, wo, g1, b1, False, 256, "outproj_ctx")
            ctx = _mlp_call(ctx, mod_ctx, w1, w2, g2, b2, False, 256, "mlp_ctx")
    return x
```
